```python
import math
import jax, jax.numpy as jnp
from jax import lax
import numpy as np

D_MODEL = 1024
BATCH = 16
SEQ = 256
DEPTH = 4
DEC_BATCH = 4
DEC_SEQ = 2048
PAST_LEN = 512

GRID_W = 64
N_EVEN = (DEPTH + 1) // 2
N_ODD = DEPTH // 2
N_MOD = 9
D_FF = 2816
H_A = 8
NOPE_A = 64
ROPE_A = 32
V_A = 64
Q_LORA = 384
KV_LORA = 256
H_B = 4
DH_B = 64
DV_B = 2 * DH_B
H_C = 4
DK_C = 64
DV_C = 128
GATE_RANK = 16
GATE_TAU = 16.0
H_D = 4
DK_D = 64
DV_D = 128
CHUNK = 64
Q_BLOCK = 128
ROPE_BASE = 10000.0
EPS = 1e-6

EVEN_SIZES = (Q_LORA, KV_LORA, ROPE_A, H_B * 2 * DH_B, H_B * 2 * DH_B, H_B * DV_B)
EVEN_CUTS = [sum(EVEN_SIZES[:i + 1]) for i in range(len(EVEN_SIZES) - 1)]
IN_EVEN = sum(EVEN_SIZES)
ODD_SIZES = (H_C * DK_C, H_C * DK_C, H_C * DV_C, 2 * GATE_RANK, H_C * DV_C,
             H_D * DK_D, H_D * DK_D, H_D * DV_D, H_D * DV_D)
ODD_CUTS = [sum(ODD_SIZES[:i + 1]) for i in range(len(ODD_SIZES) - 1)]
IN_ODD = sum(ODD_SIZES)
MIX_EVEN = H_A * V_A + H_B * DV_B
MIX_ODD = H_C * DV_C + H_D * DV_D

kernel_name = "hybrid_diffusion_prefix_step"

F32 = jnp.float32


def rmsnorm(x, g):
    xf = x.astype(F32)
    y = xf * lax.rsqrt(jnp.mean(xf * xf, axis=-1, keepdims=True) + EPS)
    return (y * g.astype(F32)).astype(x.dtype)


def group_norm(x, g):
    xf = x.astype(F32)
    mu = jnp.mean(xf, axis=-1, keepdims=True)
    var = jnp.mean(jnp.square(xf - mu), axis=-1, keepdims=True)
    return ((xf - mu) * lax.rsqrt(var + EPS) * g.astype(F32)).astype(x.dtype)


def rope_1d(x, pos):
    n = x.shape[-1]
    half = n // 2
    inv = ROPE_BASE ** (-jnp.arange(half, dtype=F32) * 2.0 / n)
    ang = pos[:, None] * inv[None, :]
    shp = (1, pos.shape[0]) + (1,) * (x.ndim - 3) + (half,)
    cos = jnp.cos(ang).reshape(shp)
    sin = jnp.sin(ang).reshape(shp)
    xf = x.astype(F32)
    x1, x2 = xf[..., :half], xf[..., half:]
    return jnp.concatenate([x1 * cos - x2 * sin, x2 * cos + x1 * sin], axis=-1).astype(x.dtype)


def rope_2d(x, row, col):
    n = x.shape[-1]
    return jnp.concatenate([rope_1d(x[..., :n // 2], row), rope_1d(x[..., n // 2:], col)], axis=-1)


def adaln(cond, w, b):
    m = jax.nn.silu(cond) @ w + b
    return m.reshape(cond.shape[0], 1, N_MOD, D_MODEL)


def modulate(x, g, shift, scale):
    return rmsnorm(x, g) * (1.0 + scale) + shift


def swiglu(x, wg, wu, wd):
    return (jax.nn.silu(x @ wg) * (x @ wu)) @ wd


def ffn_half(x, m, j, g, wg, wu, wd):
    return x + 0.5 * m[:, :, j + 2] * swiglu(modulate(x, g, m[:, :, j], m[:, :, j + 1]), wg, wu, wd)


def sweep_queries(fn, qs):
    b, t = qs[0].shape[:2]
    nb = t // Q_BLOCK
    blocks = tuple(jnp.moveaxis(q.reshape((b, nb, Q_BLOCK) + q.shape[2:]), 1, 0) for q in qs)
    out = lax.map(lambda qb: fn(*qb), blocks)
    return jnp.moveaxis(out, 0, 1).reshape((b, t) + out.shape[3:])


def mla_attend(q_nope, q_rope, k_nope, k_rope, v):
    scale = (NOPE_A + ROPE_A) ** -0.5

    def blk(qn, qr):
        s = (jnp.einsum('bqhd,bkhd->bhqk', qn, k_nope)
             + jnp.einsum('bqhr,bkr->bhqk', qr, k_rope)).astype(F32) * scale
        p = jax.nn.softmax(s, axis=-1).astype(v.dtype)
        return jnp.einsum('bhqk,bkhd->bqhd', p, v)

    return sweep_queries(blk, (q_nope, q_rope))


def diff_attend(q1, q2, k1, k2, v, lam):
    scale = DH_B ** -0.5

    def blk(q1b, q2b):
        s1 = jnp.einsum('bqhd,bkhd->bhqk', q1b, k1).astype(F32) * scale
        s2 = jnp.einsum('bqhd,bkhd->bhqk', q2b, k2).astype(F32) * scale
        p = jax.nn.softmax(s1, axis=-1) - lam * jax.nn.softmax(s2, axis=-1)
        return jnp.einsum('bhqk,bkhd->bqhd', p.astype(v.dtype), v)

    return sweep_queries(blk, (q1, q2))


def chunk_scan(q, k, v, g, s0):
    b, t, h, _ = q.shape
    nc = t // CHUNK
    g = g.astype(q.dtype)

    def to_chunks(a):
        return jnp.moveaxis(a.reshape((b, nc, CHUNK) + a.shape[2:]), 1, 0)

    mask = jnp.tril(jnp.ones((CHUNK, CHUNK), dtype=bool))[None, :, :, None, None]

    def step(s, inp):
        qc, kc, vc, gc = inp
        bc = jnp.cumsum(gc, axis=1)
        diff = bc[:, :, None] - bc[:, None]
        dec = jnp.exp(jnp.where(mask, diff, -jnp.inf))
        a = jnp.sum(qc[:, :, None] * kc[:, None] * dec, axis=-1)
        o = (jnp.einsum('btsh,bshv->bthv', a, vc)
             + jnp.einsum('bthk,bhkv->bthv', qc * jnp.exp(bc), s))
        blast = bc[:, -1]
        s_new = (jnp.exp(blast)[..., None] * s
                 + jnp.einsum('bshk,bshv->bhkv', kc * jnp.exp(blast[:, None] - bc), vc))
        return s_new.astype(s.dtype), o

    s_fin, o = lax.scan(step, s0, (to_chunks(q), to_chunks(k), to_chunks(v), to_chunks(g)))
    return jnp.moveaxis(o, 0, 1).reshape(b, t, h, v.shape[-1]), s_fin


def bidir_scan(q, k, v, g, s):
    o_f, s_f = chunk_scan(q, k, v, g[:, :, 0], s[:, 0])
    flip = lambda a: jnp.flip(a, axis=1)
    o_b, s_b = chunk_scan(flip(q), flip(k), flip(v), flip(g[:, :, 1]), s[:, 1])
    return o_f + flip(o_b), jnp.stack([s_f, s_b], axis=1)


def even_project(h, w_in, q_norm, w_uq, kv_norm):
    b, t, _ = h.shape
    cq, ckv, krope, qd, kd, vd = jnp.split(h @ w_in, EVEN_CUTS, axis=-1)
    q = (rmsnorm(cq, q_norm) @ w_uq).reshape(b, t, H_A, NOPE_A + ROPE_A)
    return (q[..., :NOPE_A], q[..., NOPE_A:], rmsnorm(ckv, kv_norm), krope,
            qd.reshape(b, t, H_B, 2 * DH_B), kd.reshape(b, t, H_B, 2 * DH_B), vd.reshape(b, t, H_B, DV_B))


def even_mix(q_nope, q_rope, ckv, krope, qd, kd, vd, w_ukv, lam_p, subln, w_out, lam_init):
    b, tk = ckv.shape[:2]
    bq, tq = q_nope.shape[:2]
    kv = (ckv @ w_ukv).reshape(b, tk, H_A, NOPE_A + V_A)
    o_a = mla_attend(q_nope, q_rope, kv[..., :NOPE_A], krope, kv[..., NOPE_A:])
    lp = lam_p.astype(F32)
    lam = jnp.exp(jnp.sum(lp[0] * lp[1])) - jnp.exp(jnp.sum(lp[2] * lp[3])) + lam_init
    o_b = diff_attend(qd[..., :DH_B], qd[..., DH_B:], kd[..., :DH_B], kd[..., DH_B:], vd, lam)
    o_b = rmsnorm(o_b, subln) * (1.0 - lam_init)
    o = jnp.concatenate([o_a.reshape(bq, tq, H_A * V_A), o_b.reshape(bq, tq, H_B * DV_B)], axis=-1)
    return o @ w_out


def odd_project(h, w_in, w_gate, b_gate, ret_decay):
    b, t, _ = h.shape
    gq, gk, gv, glr, gr, rq, rk, rv, rr = jnp.split(h @ w_in, ODD_CUTS, axis=-1)
    logit = jnp.einsum('btdr,drk->btdk', glr.reshape(b, t, 2, GATE_RANK), w_gate) + b_gate
    g_gla = (jax.nn.log_sigmoid(logit) / GATE_TAU).reshape(b, t, 2, H_C, DK_C)
    log_gamma = jnp.log1p(-jnp.exp2(-ret_decay)).astype(h.dtype)
    g_ret = jnp.broadcast_to(log_gamma[None, None, :, :, None], (b, t, 2, H_D, 1))
    gla = (gq.reshape(b, t, H_C, DK_C) * DK_C ** -0.5, gk.reshape(b, t, H_C, DK_C),
           gv.reshape(b, t, H_C, DV_C), g_gla, gr)
    ret = (rq.reshape(b, t, H_D, DK_D), rk.reshape(b, t, H_D, DK_D) * DK_D ** -0.5,
           rv.reshape(b, t, H_D, DV_D), g_ret, rr)
    return gla, ret


def odd_mix(gla, ret, s_gla, s_ret, gla_norm, ret_norm, w_out):
    gq, gk, gv, g_gla, gr = gla
    rq, rk, rv, g_ret, rr = ret
    b, t = gq.shape[:2]
    o_c, s_gla_new = bidir_scan(gq, gk, gv, g_gla, s_gla)
    o_d, s_ret_new = bidir_scan(rq, rk, rv, g_ret, s_ret)
    o_c = rmsnorm(o_c, gla_norm).reshape(b, t, H_C * DV_C) * jax.nn.silu(gr)
    o_d = group_norm(o_d, ret_norm).reshape(b, t, H_D * DV_D) * jax.nn.silu(rr)
    return jnp.concatenate([o_c, o_d], axis=-1) @ w_out, s_gla_new, s_ret_new


def setup_inputs(seed: int = 0) -> dict:
    key = jax.random.key(seed)
    ks = iter(jax.random.split(key, 40))

    def nrm(shape, scale=1.0):
        return jax.random.normal(next(ks), shape, dtype=F32) * scale

    def gain(shape):
        return 1.0 + nrm(shape, 0.02)

    return {
        "x_prompt": nrm((BATCH, SEQ, D_MODEL)),
        "x_sample": nrm((DEC_BATCH, DEC_SEQ, D_MODEL)),
        "cache_mla_ckv": nrm((DEC_BATCH, N_EVEN, PAST_LEN, KV_LORA)),
        "cache_mla_krope": nrm((DEC_BATCH, N_EVEN, PAST_LEN, ROPE_A)),
        "cache_diff_k": nrm((DEC_BATCH, N_EVEN, PAST_LEN, H_B, 2 * DH_B)),
        "cache_diff_v": nrm((DEC_BATCH, N_EVEN, PAST_LEN, H_B, DV_B)),
        "state_gla": nrm((DEC_BATCH, N_ODD, 2, H_C, DK_C, DV_C)),
        "state_ret": nrm((DEC_BATCH, N_ODD, 2, H_D, DK_D, DV_D)),
        "c": nrm((DEC_BATCH, D_MODEL)),
        "c_ctx": nrm((D_MODEL,)),
        "mod_w": nrm((DEPTH, D_MODEL, N_MOD * D_MODEL), D_MODEL ** -0.5),
        "mod_b": nrm((DEPTH, N_MOD * D_MODEL), 0.02),
        "norm_g": gain((DEPTH, 3, D_MODEL)),
        "ffn_w_gate": nrm((DEPTH, 2, D_MODEL, D_FF), D_MODEL ** -0.5),
        "ffn_w_up": nrm((DEPTH, 2, D_MODEL, D_FF), D_MODEL ** -0.5),
        "ffn_w_down": nrm((DEPTH, 2, D_FF, D_MODEL), D_FF ** -0.5),
        "even_w_in": nrm((N_EVEN, D_MODEL, IN_EVEN), D_MODEL ** -0.5),
        "mla_q_norm": gain((N_EVEN, Q_LORA)),
        "mla_w_uq": nrm((N_EVEN, Q_LORA, H_A * (NOPE_A + ROPE_A)), Q_LORA ** -0.5),
        "mla_kv_norm": gain((N_EVEN, KV_LORA)),
        "mla_w_ukv": nrm((N_EVEN, KV_LORA, H_A * (NOPE_A + V_A)), KV_LORA ** -0.5),
        "diff_lambda": nrm((N_EVEN, 4, DH_B), 0.1),
        "diff_subln": gain((N_EVEN, DV_B)),
        "even_w_out": nrm((N_EVEN, MIX_EVEN, D_MODEL), MIX_EVEN ** -0.5),
        "odd_w_in": nrm((N_ODD, D_MODEL, IN_ODD), D_MODEL ** -0.5),
        "gla_w_gate": nrm((N_ODD, 2, GATE_RANK, H_C * DK_C), GATE_RANK ** -0.5),
        "gla_b_gate": nrm((N_ODD, 2, H_C * DK_C), 0.1),
        "gla_norm": gain((N_ODD, DV_C)),
        "ret_decay": 5.0 + jnp.arange(H_D, dtype=F32)[None, None, :] + nrm((N_ODD, 2, H_D), 0.1),
        "ret_norm": gain((N_ODD, DV_D)),
        "odd_w_out": nrm((N_ODD, MIX_ODD, D_MODEL), MIX_ODD ** -0.5),
        "final_g": gain((D_MODEL,)),
    }


def reference(x_prompt, x_sample, cache_mla_ckv, cache_mla_krope, cache_diff_k, cache_diff_v,
              state_gla, state_ret, c, c_ctx, mod_w, mod_b, norm_g, ffn_w_gate, ffn_w_up, ffn_w_down,
              even_w_in, mla_q_norm, mla_w_uq, mla_kv_norm, mla_w_ukv, diff_lambda, diff_subln, even_w_out,
              odd_w_in, gla_w_gate, gla_b_gate, gla_norm, ret_decay, ret_norm, odd_w_out, final_g):
    b_p, t_p, _ = x_prompt.shape
    b_s, t_s, _ = x_sample.shape
    rows = t_s // GRID_W
    row = jnp.repeat(jnp.arange(rows), GRID_W).astype(F32)
    col = jnp.tile(jnp.arange(GRID_W), rows).astype(F32)
    tpos = jnp.arange(t_s, dtype=F32)

    xp, xs = x_prompt, x_sample
    new_ckv, new_krope, new_dk, new_dv, new_sg, new_sr = [], [], [], [], [], []
    for l in range(DEPTH):
        mp = adaln(c_ctx[None].astype(x_prompt.dtype), mod_w[l], mod_b[l])
        ms = adaln(c, mod_w[l], mod_b[l])
        xp = ffn_half(xp, mp, 0, norm_g[l, 0], ffn_w_gate[l, 0], ffn_w_up[l, 0], ffn_w_down[l, 0])
        xs = ffn_half(xs, ms, 0, norm_g[l, 0], ffn_w_gate[l, 0], ffn_w_up[l, 0], ffn_w_down[l, 0])
        hp = modulate(xp, norm_g[l, 1], mp[:, :, 3], mp[:, :, 4])
        hs = modulate(xs, norm_g[l, 1], ms[:, :, 3], ms[:, :, 4])
        i = l // 2
        if l % 2 == 0:
            lam_init = 0.8 - 0.6 * math.exp(-0.3 * l)
            qn, qr, ckv, krope, qd, kd, vd = even_project(hp, even_w_in[i], mla_q_norm[i], mla_w_uq[i], mla_kv_norm[i])
            op = even_mix(qn, qr, ckv, krope, qd, kd, vd, mla_w_ukv[i], diff_lambda[i], diff_subln[i],
                          even_w_out[i], lam_init)
            new_ckv.append(ckv)
            new_krope.append(krope)
            new_dk.append(kd)
            new_dv.append(vd)
            qn, qr, ckv, krope, qd, kd, vd = even_project(hs, even_w_in[i], mla_q_norm[i], mla_w_uq[i], mla_kv_norm[i])
            qr = rope_2d(qr, row, col)
            krope = rope_2d(krope, row, col)
            qd = rope_2d(qd.reshape(b_s, t_s, H_B, 2, DH_B), row, col).reshape(b_s, t_s, H_B, 2 * DH_B)
            kd = rope_2d(kd.reshape(b_s, t_s, H_B, 2, DH_B), row, col).reshape(b_s, t_s, H_B, 2 * DH_B)
            os_ = even_mix(qn, qr,
                           jnp.concatenate([cache_mla_ckv[:, i], ckv], axis=1),
                           jnp.concatenate([cache_mla_krope[:, i], krope], axis=1),
                           qd,
                           jnp.concatenate([cache_diff_k[:, i], kd], axis=1),
                           jnp.concatenate([cache_diff_v[:, i], vd], axis=1),
                           mla_w_ukv[i], diff_lambda[i], diff_subln[i], even_w_out[i], lam_init)
        else:
            gla, ret = odd_project(hp, odd_w_in[i], gla_w_gate[i], gla_b_gate[i], ret_decay[i])
            s_gla0 = jnp.zeros((b_p, 2, H_C, DK_C, DV_C), dtype=hp.dtype)
            s_ret0 = jnp.zeros((b_p, 2, H_D, DK_D, DV_D), dtype=hp.dtype)
            op, sg, sr = odd_mix(gla, ret, s_gla0, s_ret0, gla_norm[i], ret_norm[i], odd_w_out[i])
            new_sg.append(sg)
            new_sr.append(sr)
            gla, ret = odd_project(hs, odd_w_in[i], gla_w_gate[i], gla_b_gate[i], ret_decay[i])
            rq, rk, rv, g_ret, rr = ret
            ret = (rope_1d(rq, tpos), rope_1d(rk, tpos), rv, g_ret, rr)
            os_, _, _ = odd_mix(gla, ret, state_gla[:, i], state_ret[:, i], gla_norm[i], ret_norm[i], odd_w_out[i])
        xp = xp + mp[:, :, 5] * op
        xs = xs + ms[:, :, 5] * os_
        xp = ffn_half(xp, mp, 6, norm_g[l, 2], ffn_w_gate[l, 1], ffn_w_up[l, 1], ffn_w_down[l, 1])
        xs = ffn_half(xs, ms, 6, norm_g[l, 2], ffn_w_gate[l, 1], ffn_w_up[l, 1], ffn_w_down[l, 1])

    y_prompt = rmsnorm(xp, final_g)
    y_sample = rmsnorm(xs, final_g)
    return (y_prompt, y_sample, jnp.stack(new_ckv, axis=1), jnp.stack(new_krope, axis=1),
            jnp.stack(new_dk, axis=1), jnp.stack(new_dv, axis=1),
            jnp.stack(new_sg, axis=1), jnp.stack(new_sr, axis=1))
```

```python
import functools
import math

import numpy as np
import jax
import jax.numpy as jnp
from jax import lax
from jax.experimental import pallas as pl
from jax.experimental.pallas import tpu as pltpu

F32 = jnp.float32
BF16 = jnp.bfloat16

D_MODEL = 1024
BATCH = 16
SEQ = 256
DEPTH = 4
DEC_BATCH = 4
DEC_SEQ = 2048
PAST_LEN = 512
GRID_W = 64
N_EVEN = (DEPTH + 1) // 2
N_ODD = DEPTH // 2
N_MOD = 9
D_FF = 2816
H_A, NOPE_A, ROPE_A, V_A = 8, 64, 32, 64
Q_LORA, KV_LORA = 384, 256
H_B, DH_B = 4, 64
DV_B = 2 * DH_B
H_C, DK_C, DV_C = 4, 64, 128
GATE_RANK = 16
GATE_TAU = 16.0
H_D, DK_D, DV_D = 4, 64, 128
CHUNK = 64
ROPE_BASE = 10000.0
EPS = 1e-6

LANES = 128
SUBLANES = 8
VMEM_LIMIT_BYTES = 56 * 1024 * 1024

P_TOK = BATCH * SEQ
S_TOK = DEC_BATCH * DEC_SEQ
N_TOK = P_TOK + S_TOK
N_GROUPS = SUBLANES

TM = 512
FF_CHUNK = 256
TQ = 256
HEAD_W = LANES
EVEN_W = Q_LORA + KV_LORA + HEAD_W + 3 * H_B * 2 * DH_B
ODD_MAIN = 3072
ODD_W = ODD_MAIN + LANES
ODD_OUT = ODD_MAIN + 2 * H_C * DK_C
LEVELS = (32, 16, 8, 4, 2, 1)
N_WALL = len(LEVELS) + 2


def _dot(a, b):
    return jnp.dot(a, b, preferred_element_type=F32)


def _dot_nt(a, b):
    return lax.dot_general(a, b, (((1,), (1,)), ((), ())), preferred_element_type=F32)


def _dot_tn(a, b):
    return lax.dot_general(a, b, (((0,), (0,)), ((), ())), preferred_element_type=F32)


def _silu(x):
    return x * (1.0 / (1.0 + jnp.exp(-x)))


def _rms(x, g):
    return x * lax.rsqrt(jnp.mean(x * x, axis=-1, keepdims=True) + EPS) * g


def _modulate(x, g, shift, scale):
    return _rms(x, g) * (1.0 + scale) + shift


def _group_of_block(i, tm):
    return jnp.where(i < P_TOK // tm, 0, 1 + (i - P_TOK // tm) // (DEC_SEQ // tm))


def _cparams(*sem):
    return pltpu.CompilerParams(dimension_semantics=sem, vmem_limit_bytes=VMEM_LIMIT_BYTES)


def _resident(shape, index_map):
    return pl.BlockSpec(shape, index_map, pipeline_mode=pl.Buffered(1))


def _adaln_kernel(c_ref, w_ref, b_ref, o_ref):
    s = _silu(c_ref[...]).astype(BF16)
    o_ref[...] = _dot(s, w_ref[...].astype(BF16)) + b_ref[...]


def _adaln(cond8, mod_w, mod_b):
    mod_b4 = mod_b.reshape(DEPTH, N_MOD, 1, D_MODEL)
    return pl.pallas_call(
        _adaln_kernel,
        grid=(DEPTH, N_MOD),
        in_specs=[
            pl.BlockSpec((N_GROUPS, D_MODEL), lambda l, j: (0, 0)),
            pl.BlockSpec((None, D_MODEL, D_MODEL), lambda l, j: (l, 0, j)),
            pl.BlockSpec((None, None, 1, D_MODEL), lambda l, j: (l, j, 0, 0)),
        ],
        out_specs=pl.BlockSpec((None, None, N_GROUPS, D_MODEL), lambda l, j: (l, j, 0, 0)),
        out_shape=jax.ShapeDtypeStruct((DEPTH, N_MOD, N_GROUPS, D_MODEL), F32),
        compiler_params=_cparams("arbitrary", "arbitrary"),
        name="adaln",
    )(cond8, mod_w, mod_b4)


def _mod_spec(l, j, grp):
    return pl.BlockSpec((None, None, None, 1, D_MODEL), lambda i: (l, j, grp(i), 0, 0))


def _ffn_kernel(*refs, has_mix, final):
    it = iter(refs)
    x_ref = next(it)
    if has_mix:
        mix_ref, wout_ref, gmix_ref = next(it), next(it), next(it)
    sh_ref, sc_ref, gt_ref, ng_ref = next(it), next(it), next(it), next(it)
    wg_ref, wu_ref, wd_ref = next(it), next(it), next(it)
    if final:
        fg_ref = next(it)
    o_ref = next(it)
    if final:
        y_ref = next(it)

    x = x_ref[...]
    if has_mix:
        x = x + gmix_ref[...] * _dot(mix_ref[...], wout_ref[...])
    h = _modulate(x, ng_ref[...], sh_ref[...], sc_ref[...]).astype(BF16)
    acc = jnp.zeros(x.shape, F32)
    for j in range(D_FF // FF_CHUNK):
        cols = slice(j * FF_CHUNK, (j + 1) * FF_CHUNK)
        a = _dot(h, wg_ref[:, cols])
        u = _dot(h, wu_ref[:, cols])
        acc = acc + _dot((_silu(a) * u).astype(BF16), wd_ref[cols, :])
    out = x + 0.5 * gt_ref[...] * acc
    o_ref[...] = out
    if final:
        y_ref[...] = _rms(out, fg_ref[...])


def _ffn(x, mod5, norm_g3, wg, wu, wd, l, k, *, mix=None, w_out=None, w_out_idx=None, final_g=None):
    j = 0 if k == 0 else 6
    grp = functools.partial(_group_of_block, tm=TM)
    row_spec = lambda w: pl.BlockSpec((TM, w), lambda i: (i, 0))
    in_specs = [row_spec(D_MODEL)]
    args = [x]
    if mix is not None:
        in_specs += [row_spec(D_MODEL),
                     _resident((None, D_MODEL, D_MODEL), lambda i: (w_out_idx, 0, 0)),
                     _mod_spec(l, 5, grp)]
        args += [mix, w_out, mod5]
    in_specs += [_mod_spec(l, j, grp), _mod_spec(l, j + 1, grp), _mod_spec(l, j + 2, grp),
                 pl.BlockSpec((None, 1, D_MODEL), lambda i: (3 * l + (0 if k == 0 else 2), 0, 0)),
                 _resident((None, None, D_MODEL, D_FF), lambda i: (l, k, 0, 0)),
                 _resident((None, None, D_MODEL, D_FF), lambda i: (l, k, 0, 0)),
                 _resident((None, None, D_FF, D_MODEL), lambda i: (l, k, 0, 0))]
    args += [mod5, mod5, mod5, norm_g3, wg, wu, wd]
    out_specs = [row_spec(D_MODEL)]
    out_shape = [jax.ShapeDtypeStruct((N_TOK, D_MODEL), F32)]
    if final_g is not None:
        in_specs.append(pl.BlockSpec((1, D_MODEL), lambda i: (0, 0)))
        args.append(final_g)
        out_specs.append(row_spec(D_MODEL))
        out_shape.append(jax.ShapeDtypeStruct((N_TOK, D_MODEL), F32))
    res = pl.pallas_call(
        functools.partial(_ffn_kernel, has_mix=mix is not None, final=final_g is not None),
        grid=(N_TOK // TM,),
        in_specs=in_specs,
        out_specs=out_specs,
        out_shape=out_shape,
        compiler_params=_cparams("parallel"),
        name=f"ffn_l{l}_{k}",
    )(*args)
    return res if final_g is not None else res[0]


def _rope_tables(subvectors, positions):
    t = next(iter(positions.values())).shape[0]
    cos = jnp.ones((t, LANES), F32)
    sin_m = jnp.zeros((t, LANES), F32)
    sin_p = jnp.zeros((t, LANES), F32)
    for first, n, key in subvectors:
        half = n // 2
        inv = ROPE_BASE ** (-jnp.arange(half, dtype=F32) * 2.0 / n)
        ang = positions[key][:, None] * inv[None, :]
        c, s = jnp.cos(ang), jnp.sin(ang)
        cos = cos.at[:, first:first + n].set(jnp.concatenate([c, c], axis=1))
        sin_m = sin_m.at[:, first:first + half].set(-s)
        sin_p = sin_p.at[:, first + half:first + n].set(s)
    return cos, sin_m, sin_p


def _tile_lanes(t, width):
    return t if width == LANES else jnp.concatenate([t] * (width // LANES), axis=1)


def _rope(y, cos, sin_m, sin_p, half):
    w = y.shape[1]
    c, sm, sp = (_tile_lanes(t, w) for t in (cos, sin_m, sin_p))
    return y * c + pltpu.roll(y, w - half, 1) * sm + pltpu.roll(y, half, 1) * sp


def _even_proj_kernel(*refs, rope, cache):
    it = iter(refs)
    x_ref, sh_ref, sc_ref, ng_ref = next(it), next(it), next(it), next(it)
    win_ref, qn_ref, wuq_ref, kvn_ref, wk_ref, wv_ref = (next(it) for _ in range(6))
    if rope:
        tq = [next(it)[...] for _ in range(3)]
        td = [next(it)[...] for _ in range(3)]
    q_ref, k_ref, va_ref, qd_ref, kd_ref, vd_ref = (next(it) for _ in range(6))
    if cache:
        ckv_ref, krp_ref, kd32_ref, vd32_ref = (next(it) for _ in range(4))

    h = _modulate(x_ref[...], ng_ref[...], sh_ref[...], sc_ref[...]).astype(BF16)
    p = _dot(h, win_ref[...])
    o_ckv = Q_LORA
    o_krp = o_ckv + KV_LORA
    o_qd = o_krp + HEAD_W
    o_kd = o_qd + H_B * 2 * DH_B
    o_vd = o_kd + H_B * 2 * DH_B
    cqn = _rms(p[:, :Q_LORA], qn_ref[...]).astype(BF16)
    q = _dot(cqn, wuq_ref[...])
    ckvn = _rms(p[:, o_ckv:o_krp], kvn_ref[...])
    ckvb = ckvn.astype(BF16)
    krp = p[:, o_krp:o_qd]
    qd = p[:, o_qd:o_kd]
    kd = p[:, o_kd:o_vd]
    vd = p[:, o_vd:]
    if cache:
        ckv_ref[...] = ckvn
        krp_ref[...] = krp
        kd32_ref[...] = kd
        vd32_ref[...] = vd
    if rope:
        q = _rope(q, *tq, half=ROPE_A // 4)
        krp = _rope(krp, *tq, half=ROPE_A // 4)
        qd = _rope(qd, *td, half=DH_B // 4)
        kd = _rope(kd, *td, half=DH_B // 4)
    q_ref[...] = (q * ((NOPE_A + ROPE_A) ** -0.5)).astype(BF16)
    k_ref[...] = (_dot(ckvb, wk_ref[...]) + _tile_lanes(krp, H_A * HEAD_W)).astype(BF16)
    va_ref[...] = _dot(ckvb, wv_ref[...]).astype(BF16)
    qd_ref[...] = (qd * (DH_B ** -0.5)).astype(BF16)
    kd_ref[...] = kd.astype(BF16)
    vd_ref[...] = vd.astype(BF16)


def _even_proj(x, mod5, norm_g3, w_in, q_norm, w_uq, kv_norm, w_k, w_v, l, *, latent, tables=None):
    i = l // 2
    n_rows = S_TOK if latent else P_TOK
    first = P_TOK // TM if latent else 0
    grp = (lambda b: 1 + b // (DEC_SEQ // TM)) if latent else (lambda b: 0)
    row_spec = lambda w: pl.BlockSpec((TM, w), lambda b: (b, 0))
    const = lambda shape: _resident((None,) + shape, lambda b: (i,) + (0,) * len(shape))
    in_specs = [pl.BlockSpec((TM, D_MODEL), lambda b: (first + b, 0)),
                _mod_spec(l, 3, grp), _mod_spec(l, 4, grp),
                pl.BlockSpec((None, 1, D_MODEL), lambda b: (3 * l + 1, 0, 0)),
                const((D_MODEL, EVEN_W)), const((1, Q_LORA)), const((Q_LORA, H_A * HEAD_W)),
                const((1, KV_LORA)), const((KV_LORA, H_A * HEAD_W)), const((KV_LORA, H_A * V_A))]
    args = [x, mod5, mod5, norm_g3, w_in, q_norm, w_uq, kv_norm, w_k, w_v]
    if latent:
        in_specs += [pl.BlockSpec((TM, LANES), lambda b: (b % (DEC_SEQ // TM), 0))] * 6
        args += list(tables)
    widths = [H_A * HEAD_W, H_A * HEAD_W, H_A * V_A, H_B * DV_B, H_B * DV_B, H_B * DV_B]
    out_specs = [row_spec(w) for w in widths]
    out_shape = [jax.ShapeDtypeStruct((n_rows, w), BF16) for w in widths]
    if not latent:
        cache_w = [KV_LORA, HEAD_W, H_B * DV_B, H_B * DV_B]
        out_specs += [row_spec(w) for w in cache_w]
        out_shape += [jax.ShapeDtypeStruct((n_rows, w), F32) for w in cache_w]
    return pl.pallas_call(
        functools.partial(_even_proj_kernel, rope=latent, cache=not latent),
        grid=(n_rows // TM,),
        in_specs=in_specs,
        out_specs=out_specs,
        out_shape=out_shape,
        compiler_params=_cparams("parallel"),
        name=f"even_proj_l{l}_{'s' if latent else 'p'}",
    )(*args)


def _cache_kv_kernel(ckv_ref, krp_ref, wk_ref, wv_ref, k_ref, va_ref):
    c = ckv_ref[...].astype(BF16)
    k_ref[...] = (_dot(c, wk_ref[...]) + _tile_lanes(krp_ref[...], H_A * HEAD_W)).astype(BF16)
    va_ref[...] = _dot(c, wv_ref[...]).astype(BF16)


def _cache_kv(ckv, krp, w_k, w_v, i):
    n = DEC_BATCH * PAST_LEN
    return pl.pallas_call(
        _cache_kv_kernel,
        grid=(n // PAST_LEN,),
        in_specs=[pl.BlockSpec((PAST_LEN, KV_LORA), lambda b: (b, 0)),
                  pl.BlockSpec((PAST_LEN, HEAD_W), lambda b: (b, 0)),
                  _resident((None, KV_LORA, H_A * HEAD_W), lambda b: (i, 0, 0)),
                  _resident((None, KV_LORA, H_A * V_A), lambda b: (i, 0, 0))],
        out_specs=[pl.BlockSpec((PAST_LEN, H_A * HEAD_W), lambda b: (b, 0)),
                   pl.BlockSpec((PAST_LEN, H_A * V_A), lambda b: (b, 0))],
        out_shape=[jax.ShapeDtypeStruct((n, H_A * HEAD_W), BF16),
                   jax.ShapeDtypeStruct((n, H_A * V_A), BF16)],
        compiler_params=_cparams("parallel"),
        name=f"cache_kv_{i}",
    )(ckv, krp, w_k, w_v)


def _softmax_parts(s):
    m = jnp.max(s, axis=-1, keepdims=True)
    p = jnp.exp(s - m)
    return p, 1.0 / jnp.sum(p, axis=-1, keepdims=True)


def _attn_kernel(q_ref, k_ref, va_ref, qd_ref, kd_ref, vd_ref, lam_ref, sub_ref, o_ref, *, lam_init):
    outs = []
    for h in range(H_A):
        cols = slice(h * HEAD_W, (h + 1) * HEAD_W)
        p, inv = _softmax_parts(_dot_nt(q_ref[:, cols], k_ref[:, cols]))
        outs.append(_dot(p.astype(BF16), va_ref[:, h * V_A:(h + 1) * V_A]) * inv)
    lp = lam_ref[...]
    lam = (jnp.exp(jnp.sum(lp[0:1] * lp[1:2], axis=-1, keepdims=True))
           - jnp.exp(jnp.sum(lp[2:3] * lp[3:4], axis=-1, keepdims=True)) + lam_init)
    first = lax.broadcasted_iota(jnp.int32, (1, DV_B), 1) < DH_B
    for h in range(H_B):
        cols = slice(h * DV_B, (h + 1) * DV_B)
        qh = qd_ref[:, cols].astype(F32)
        kh = kd_ref[:, cols]
        p1, inv1 = _softmax_parts(_dot_nt(jnp.where(first, qh, 0.0).astype(BF16), kh))
        p2, inv2 = _softmax_parts(_dot_nt(jnp.where(first, 0.0, qh).astype(BF16), kh))
        p = p1 * inv1 - p2 * (lam * inv2)
        o = _dot(p.astype(BF16), vd_ref[:, cols])
        outs.append(_rms(o, sub_ref[...]) * (1.0 - lam_init))
    o_ref[...] = jnp.concatenate(outs, axis=1).astype(BF16)


def _attention(q, k, va, qd, kd, vd, lam_p, subln, i, lam_init, *, n_batch, t_q, t_k, name):
    nq = t_q // TQ
    q_spec = lambda w: pl.BlockSpec((TQ, w), lambda b, j: (b * nq + j, 0))
    k_spec = lambda w: pl.BlockSpec((None, t_k, w), lambda b, j: (b, 0, 0))
    return pl.pallas_call(
        functools.partial(_attn_kernel, lam_init=lam_init),
        grid=(n_batch, nq),
        in_specs=[q_spec(H_A * HEAD_W), k_spec(H_A * HEAD_W), k_spec(H_A * V_A),
                  q_spec(H_B * DV_B), k_spec(H_B * DV_B), k_spec(H_B * DV_B),
                  pl.BlockSpec((None, 4, DH_B), lambda b, j: (i, 0, 0)),
                  pl.BlockSpec((None, 1, DV_B), lambda b, j: (i, 0, 0))],
        out_specs=q_spec(D_MODEL),
        out_shape=jax.ShapeDtypeStruct((n_batch * t_q, D_MODEL), BF16),
        compiler_params=_cparams("parallel", "arbitrary"),
        name=name,
    )(q, k, va, qd, kd, vd, lam_p, subln)


def _odd_proj_kernel(*refs, rope):
    it = iter(refs)
    x_ref, sh_ref, sc_ref, ng_ref, win_ref, wgate_ref, bgate_ref = (next(it) for _ in range(7))
    if rope:
        tr = [next(it)[...] for _ in range(3)]
    o_ref = next(it)

    h = _modulate(x_ref[...], ng_ref[...], sh_ref[...], sc_ref[...]).astype(BF16)
    p = _dot(h, win_ref[...])
    o_ref[:, 0:256] = p[:, 0:256] * (DK_C ** -0.5)
    o_ref[:, 256:1536] = p[:, 256:1536]
    rq = p[:, 1536:1792]
    rk = p[:, 1792:2048] * (DK_D ** -0.5)
    if rope:
        rq = _rope(rq, *tr, half=DK_D // 2)
        rk = _rope(rk, *tr, half=DK_D // 2)
    o_ref[:, 1536:1792] = rq
    o_ref[:, 1792:2048] = rk
    o_ref[:, 2048:ODD_MAIN] = p[:, 2048:ODD_MAIN]
    logit = _dot(p[:, ODD_MAIN:].astype(BF16), wgate_ref[...]) + bgate_ref[...]
    o_ref[:, ODD_MAIN:] = (jnp.minimum(logit, 0.0) - jnp.log1p(jnp.exp(-jnp.abs(logit)))) * (1.0 / GATE_TAU)


def _odd_proj(x, mod5, norm_g3, w_in, w_gate, b_gate, l, *, latent, tables=None):
    i = l // 2
    n_rows = S_TOK if latent else P_TOK
    first = P_TOK // TM if latent else 0
    grp = (lambda b: 1 + b // (DEC_SEQ // TM)) if latent else (lambda b: 0)
    in_specs = [pl.BlockSpec((TM, D_MODEL), lambda b: (first + b, 0)),
                _mod_spec(l, 3, grp), _mod_spec(l, 4, grp),
                pl.BlockSpec((None, 1, D_MODEL), lambda b: (3 * l + 1, 0, 0)),
                _resident((None, D_MODEL, ODD_W), lambda b: (i, 0, 0)),
                _resident((None, LANES, 2 * H_C * DK_C), lambda b: (i, 0, 0)),
                _resident((None, 1, 2 * H_C * DK_C), lambda b: (i, 0, 0))]
    args = [x, mod5, mod5, norm_g3, w_in, w_gate, b_gate]
    if latent:
        in_specs += [pl.BlockSpec((TM, LANES), lambda b: (b % (DEC_SEQ // TM), 0))] * 3
        args += list(tables)
    return pl.pallas_call(
        functools.partial(_odd_proj_kernel, rope=latent),
        grid=(n_rows // TM,),
        in_specs=in_specs,
        out_specs=pl.BlockSpec((TM, ODD_OUT), lambda b: (b, 0)),
        out_shape=jax.ShapeDtypeStruct((n_rows, ODD_OUT), F32),
        compiler_params=_cparams("parallel"),
        name=f"odd_proj_l{l}_{'s' if latent else 'p'}",
    )(*args)


def _scan_weights():
    c = CHUNK
    t = np.arange(c)
    out = np.zeros((2, N_WALL * c, c), np.float32)
    for d in range(2):
        run = (t[None, :] <= t[:, None]) if d == 0 else (t[None, :] >= t[:, None])
        run = run.astype(np.float32)
        out[d, 0:c] = run
        for li, b in enumerate(LEVELS):
            base = (t // (2 * b)) * (2 * b)
            r = base + (b - 1 if d == 0 else b)
            out[d, (li + 1) * c:(li + 2) * c] = run - run[r]
        out[d, (N_WALL - 1) * c:] = 1.0 - run
    return out


def _scan_kernel(*refs, n_chunks, has_s0, emit_state):
    it = iter(refs)
    lg_ref = next(it)
    ins = [[next(it) for _ in range(7)] for _ in range(2)]
    wall_ref = next(it)
    if has_s0:
        s0g_ref, s0r_ref = next(it), next(it)
    o_refs = [[next(it), next(it)] for _ in range(2)]
    if emit_state:
        sgo_ref, sro_ref = next(it), next(it)
    sg_ref, sr_ref = next(it), next(it)
    c_id = pl.program_id(1)

    @pl.when(c_id == 0)
    def _():
        if has_s0:
            sg_ref[...] = s0g_ref[...]
            sr_ref[...] = s0r_ref[...]
        else:
            sg_ref[...] = jnp.zeros(sg_ref.shape, F32)
            sr_ref[...] = jnp.zeros(sr_ref.shape, F32)

    row = lax.broadcasted_iota(jnp.int32, (CHUNK, 1), 0)
    col = lax.broadcasted_iota(jnp.int32, (1, CHUNK), 1)
    dist = jnp.abs(row - col).astype(F32)
    for d in range(2):
        gq_ref, gk_ref, gv_ref, gg_ref, rq_ref, rk_ref, rv_ref = ins[d]
        causal = (row >= col) if d == 0 else (row <= col)
        q, k, v, g = gq_ref[...], gk_ref[...], gv_ref[...], gg_ref[...]
        g_hi = g.astype(BF16)
        g_lo = (g - g_hi.astype(F32)).astype(BF16)
        wall = wall_ref[d]
        dall = _dot(wall, g_hi) + _dot(wall, g_lo)
        bc = dall[0:CHUNK]
        rem = dall[(N_WALL - 1) * CHUNK:]
        tot = bc[CHUNK - 1:CHUNK] if d == 0 else bc[0:1]
        amat = [None] * H_C
        for h in range(H_C):
            hc = slice(h * DK_C, (h + 1) * DK_C)
            amat[h] = jnp.where(row == col, jnp.sum(q[:, hc] * k[:, hc], axis=-1, keepdims=True), 0.0)
        for li, b in enumerate(LEVELS):
            e = jnp.exp(-jnp.abs(dall[(li + 1) * CHUNK:(li + 2) * CHUNK]))
            later = (row & (2 * b - 1)) >= b
            is_q = later if d == 0 else jnp.logical_not(later)
            ql = jnp.where(is_q, q * e, 0.0).astype(BF16)
            kl = jnp.where(is_q, 0.0, k * e).astype(BF16)
            shift = (2 * b).bit_length() - 1
            same = (row >> shift) == (col >> shift)
            for h in range(H_C):
                hc = slice(h * DK_C, (h + 1) * DK_C)
                amat[h] = amat[h] + jnp.where(same, _dot_nt(ql[:, hc], kl[:, hc]), 0.0)
        qb = (q * jnp.exp(bc)).astype(BF16)
        kb = (k * jnp.exp(rem)).astype(BF16)
        et = jnp.exp(tot)
        outs = []
        for h in range(H_C):
            hc = slice(h * DK_C, (h + 1) * DK_C)
            vh = v[:, h * DV_C:(h + 1) * DV_C].astype(BF16)
            st = sg_ref[d, h]
            outs.append(_dot(amat[h].astype(BF16), vh) + _dot_nt(qb[:, hc], st.astype(BF16)))
            sg_ref[d, h] = st * et[:, hc] + _dot_tn(vh, kb[:, hc])
        o_refs[d][0][...] = jnp.concatenate(outs, axis=1)
        rq, rk, rv = rq_ref[...], rk_ref[...], rv_ref[...]
        cnt = ((row + 1) if d == 0 else (CHUNK - row)).astype(F32)
        outs = []
        for h in range(H_D):
            hc = slice(h * DK_D, (h + 1) * DK_D)
            lg = lg_ref[d, h]
            dec = jnp.where(causal, jnp.exp(lg * dist), 0.0)
            vh = rv[:, h * DV_D:(h + 1) * DV_D].astype(BF16)
            st = sr_ref[d, h]
            a = _dot_nt(rq[:, hc].astype(BF16), rk[:, hc].astype(BF16)) * dec
            qs = (rq[:, hc] * jnp.exp(lg * cnt)).astype(BF16)
            ks = (rk[:, hc] * jnp.exp(lg * (CHUNK - cnt))).astype(BF16)
            outs.append(_dot(a.astype(BF16), vh) + _dot_nt(qs, st.astype(BF16)))
            sr_ref[d, h] = st * jnp.exp(lg * CHUNK) + _dot_tn(vh, ks)
        o_refs[d][1][...] = jnp.concatenate(outs, axis=1)

    if emit_state:
        @pl.when(c_id == n_chunks - 1)
        def _():
            sgo_ref[...] = sg_ref[...]
            sro_ref[...] = sr_ref[...]


def _scan(p, log_gamma, wall, *, n_seq, t_seq, s0=None, emit_state):
    nc = t_seq // CHUNK
    fwd = lambda blk: (lambda b, c: (b * nc + c, blk))
    bwd = lambda blk: (lambda b, c: (b * nc + nc - 1 - c, blk))
    in_specs = [pl.BlockSpec(memory_space=pltpu.SMEM)]
    args = [log_gamma]
    for m in (fwd, bwd):
        d = 0 if m is fwd else 1
        in_specs += [pl.BlockSpec((CHUNK, 256), m(0)), pl.BlockSpec((CHUNK, 256), m(1)),
                     pl.BlockSpec((CHUNK, 512), m(1)), pl.BlockSpec((CHUNK, 256), m(ODD_MAIN // 256 + d)),
                     pl.BlockSpec((CHUNK, 256), m(6)), pl.BlockSpec((CHUNK, 256), m(7)),
                     pl.BlockSpec((CHUNK, 512), m(4))]
        args += [p] * 7
    in_specs.append(_resident((2, N_WALL * CHUNK, CHUNK), lambda b, c: (0, 0, 0)))
    args.append(wall)
    st_shape = (2, H_C, DV_C, DK_C)
    st_spec = pl.BlockSpec((None,) + st_shape, lambda b, c: (b, 0, 0, 0, 0))
    if s0 is not None:
        in_specs += [st_spec, st_spec]
        args += list(s0)
    out_specs = [pl.BlockSpec((CHUNK, 512), fwd(0)), pl.BlockSpec((CHUNK, 512), fwd(0)),
                 pl.BlockSpec((CHUNK, 512), bwd(0)), pl.BlockSpec((CHUNK, 512), bwd(0))]
    out_shape = [jax.ShapeDtypeStruct((n_seq * t_seq, 512), F32)] * 4
    if emit_state:
        out_specs += [st_spec, st_spec]
        out_shape += [jax.ShapeDtypeStruct((n_seq,) + st_shape, F32)] * 2
    return pl.pallas_call(
        functools.partial(_scan_kernel, n_chunks=nc, has_s0=s0 is not None, emit_state=emit_state),
        grid=(n_seq, nc),
        in_specs=in_specs,
        out_specs=out_specs,
        out_shape=out_shape,
        scratch_shapes=[pltpu.VMEM(st_shape, F32), pltpu.VMEM(st_shape, F32)],
        compiler_params=_cparams("parallel", "arbitrary"),
        name=f"scan_{'s' if s0 is not None else 'p'}",
    )(*args)


def _odd_post_kernel(ogf_ref, orf_ref, ogb_ref, orb_ref, gr_ref, rr_ref, gn_ref, rn_ref, o_ref):
    og = ogf_ref[...] + ogb_ref[...]
    orr = orf_ref[...] + orb_ref[...]
    gr, rr = gr_ref[...], rr_ref[...]
    outs = []
    for h in range(H_C):
        hc = slice(h * DV_C, (h + 1) * DV_C)
        outs.append(_rms(og[:, hc], gn_ref[...]) * _silu(gr[:, hc]))
    for h in range(H_D):
        hc = slice(h * DV_D, (h + 1) * DV_D)
        xh = orr[:, hc]
        mu = jnp.mean(xh, axis=-1, keepdims=True)
        xc = xh - mu
        var = jnp.mean(xc * xc, axis=-1, keepdims=True)
        outs.append(xc * lax.rsqrt(var + EPS) * rn_ref[...] * _silu(rr[:, hc]))
    o_ref[...] = jnp.concatenate(outs, axis=1).astype(BF16)


def _odd_post(scan_outs, p, gla_norm, ret_norm, i):
    n = p.shape[0]
    row = lambda blk: pl.BlockSpec((TM, 512), lambda b: (b, blk))
    return pl.pallas_call(
        _odd_post_kernel,
        grid=(n // TM,),
        in_specs=[row(0)] * 4 + [row(2), row(5),
                                 pl.BlockSpec((None, 1, DV_C), lambda b: (i, 0, 0)),
                                 pl.BlockSpec((None, 1, DV_D), lambda b: (i, 0, 0))],
        out_specs=pl.BlockSpec((TM, D_MODEL), lambda b: (b, 0)),
        out_shape=jax.ShapeDtypeStruct((n, D_MODEL), BF16),
        compiler_params=_cparams("parallel"),
        name="odd_post",
    )(*scan_outs, p, p, gla_norm, ret_norm)


def _pad_cols(w, width):
    return jnp.pad(w, [(0, 0)] * (w.ndim - 1) + [(0, width - w.shape[-1])])


def _even_weights(even_w_in, mla_w_uq, mla_w_ukv):
    cq, ckv, kr, qd, kd, vd = jnp.split(even_w_in, np.cumsum([Q_LORA, KV_LORA, ROPE_A, 512, 512]).tolist(), axis=-1)
    zeros = lambda n: jnp.zeros(even_w_in.shape[:-1] + (n,), even_w_in.dtype)
    w_in = jnp.concatenate([cq, ckv, zeros(NOPE_A), kr, zeros(HEAD_W - NOPE_A - ROPE_A), qd, kd, vd], axis=-1)
    uq = mla_w_uq.reshape(N_EVEN, Q_LORA, H_A, NOPE_A + ROPE_A)
    w_uq = _pad_cols(uq, HEAD_W).reshape(N_EVEN, Q_LORA, H_A * HEAD_W)
    ukv = mla_w_ukv.reshape(N_EVEN, KV_LORA, H_A, NOPE_A + V_A)
    w_k = _pad_cols(ukv[..., :NOPE_A], HEAD_W).reshape(N_EVEN, KV_LORA, H_A * HEAD_W)
    w_v = ukv[..., NOPE_A:].reshape(N_EVEN, KV_LORA, H_A * V_A)
    return tuple(w.astype(BF16) for w in (w_in, w_uq, w_k, w_v))


def _odd_weights(odd_w_in, gla_w_gate, gla_b_gate):
    cuts = np.cumsum([256, 256, 512, 2 * GATE_RANK, 512, 256, 256, 512]).tolist()
    gq, gk, gv, glr, gr, rq, rk, rv, rr = jnp.split(odd_w_in, cuts, axis=-1)
    w_in = jnp.concatenate([gq, gk, gv, gr, rq, rk, rv, rr, _pad_cols(glr, LANES)], axis=-1)
    w_gate = jnp.zeros((N_ODD, LANES, 2 * H_C * DK_C), F32)
    for d in range(2):
        w_gate = w_gate.at[:, d * GATE_RANK:(d + 1) * GATE_RANK, d * 256:(d + 1) * 256].set(gla_w_gate[:, d])
    b_gate = gla_b_gate.reshape(N_ODD, 1, 2 * H_C * DK_C)
    return w_in.astype(BF16), w_gate.astype(BF16), b_gate


def kernel(x_prompt, x_sample, cache_mla_ckv, cache_mla_krope, cache_diff_k, cache_diff_v, state_gla, state_ret, c, c_ctx, mod_w, mod_b, norm_g, ffn_w_gate, ffn_w_up, ffn_w_down, even_w_in, mla_q_norm, mla_w_uq, mla_kv_norm, mla_w_ukv, diff_lambda, diff_subln, even_w_out, odd_w_in, gla_w_gate, gla_b_gate, gla_norm, ret_decay, ret_norm, odd_w_out, final_g):
    x = jnp.concatenate([x_prompt.reshape(P_TOK, D_MODEL), x_sample.reshape(S_TOK, D_MODEL)], axis=0)
    cond8 = jnp.concatenate([c_ctx[None], c, jnp.zeros((N_GROUPS - 1 - DEC_BATCH, D_MODEL), F32)], axis=0)
    mod5 = _adaln(cond8, mod_w, mod_b).reshape(DEPTH, N_MOD, N_GROUPS, 1, D_MODEL)
    norm_g3 = norm_g.reshape(DEPTH * 3, 1, D_MODEL)
    wg, wu, wd = ffn_w_gate.astype(BF16), ffn_w_up.astype(BF16), ffn_w_down.astype(BF16)
    e_w_in, e_w_uq, e_w_k, e_w_v = _even_weights(even_w_in, mla_w_uq, mla_w_ukv)
    o_w_in, o_w_gate, o_b_gate = _odd_weights(odd_w_in, gla_w_gate, gla_b_gate)
    e_w_out, o_w_out = even_w_out.astype(BF16), odd_w_out.astype(BF16)
    q_norm = mla_q_norm.reshape(N_EVEN, 1, Q_LORA)
    kv_norm = mla_kv_norm.reshape(N_EVEN, 1, KV_LORA)
    subln = diff_subln.reshape(N_EVEN, 1, DV_B)
    gla_n = gla_norm.reshape(N_ODD, 1, DV_C)
    ret_n = ret_norm.reshape(N_ODD, 1, DV_D)
    log_gamma = jnp.log1p(-jnp.exp2(-ret_decay))
    wall = jnp.asarray(_scan_weights(), BF16)

    tpos = jnp.arange(DEC_SEQ, dtype=F32)
    pos = {"row": jnp.floor(tpos / GRID_W), "col": tpos % GRID_W, "t": tpos}
    tab_q = _rope_tables([(NOPE_A, ROPE_A // 2, "row"), (NOPE_A + ROPE_A // 2, ROPE_A // 2, "col")], pos)
    tab_d = _rope_tables([(o, DH_B // 2, key) for o, key in ((0, "row"), (32, "col"), (64, "row"), (96, "col"))], pos)
    tab_r = _rope_tables([(0, DK_D, "t"), (DK_D, DK_D, "t")], pos)

    mix = None
    w_out, w_out_idx = None, None
    new_ckv, new_krope, new_dk, new_dv, new_sg, new_sr = [], [], [], [], [], []
    for l in range(DEPTH):
        i = l // 2
        x = _ffn(x, mod5, norm_g3, wg, wu, wd, l, 0)
        if l % 2 == 0:
            lam_init = 0.8 - 0.6 * math.exp(-0.3 * l)
            qp, kp, vap, qdp, kdp, vdp, ckv, krp, kd32, vd32 = _even_proj(
                x, mod5, norm_g3, e_w_in, q_norm, e_w_uq, kv_norm, e_w_k, e_w_v, l, latent=False)
            new_ckv.append(ckv.reshape(BATCH, SEQ, KV_LORA))
            new_krope.append(krp[:, NOPE_A:NOPE_A + ROPE_A].reshape(BATCH, SEQ, ROPE_A))
            new_dk.append(kd32.reshape(BATCH, SEQ, H_B, 2 * DH_B))
            new_dv.append(vd32.reshape(BATCH, SEQ, H_B, DV_B))
            b3 = lambda a: a.reshape(BATCH, SEQ, a.shape[-1])
            mix_p = _attention(qp, b3(kp), b3(vap), qdp, b3(kdp), b3(vdp), diff_lambda, subln, i, lam_init,
                               n_batch=BATCH, t_q=SEQ, t_k=SEQ, name=f"attn_l{l}_p")
            qs, ks, vas, qds, kds, vds = _even_proj(
                x, mod5, norm_g3, e_w_in, q_norm, e_w_uq, kv_norm, e_w_k, e_w_v, l, latent=True,
                tables=tab_q + tab_d)
            krp_c = jnp.pad(cache_mla_krope[:, i].reshape(DEC_BATCH * PAST_LEN, ROPE_A),
                            ((0, 0), (NOPE_A, HEAD_W - NOPE_A - ROPE_A)))
            kc, vac = _cache_kv(cache_mla_ckv[:, i].reshape(DEC_BATCH * PAST_LEN, KV_LORA), krp_c, e_w_k, e_w_v, i)
            cat = lambda old, new: jnp.concatenate(
                [old.reshape(DEC_BATCH, PAST_LEN, -1), new.reshape(DEC_BATCH, DEC_SEQ, -1)], axis=1)
            mix_s = _attention(qs, cat(kc, ks), cat(vac, vas), qds,
                               cat(cache_diff_k[:, i].astype(BF16), kds), cat(cache_diff_v[:, i].astype(BF16), vds),
                               diff_lambda, subln, i, lam_init,
                               n_batch=DEC_BATCH, t_q=DEC_SEQ, t_k=PAST_LEN + DEC_SEQ, name=f"attn_l{l}_s")
            w_out, w_out_idx = e_w_out, i
        else:
            pp = _odd_proj(x, mod5, norm_g3, o_w_in, o_w_gate, o_b_gate, l, latent=False)
            *outs_p, sg, sr = _scan(pp, log_gamma[i], wall, n_seq=BATCH, t_seq=SEQ, emit_state=True)
            new_sg.append(jnp.swapaxes(sg, -1, -2))
            new_sr.append(jnp.swapaxes(sr, -1, -2))
            mix_p = _odd_post(outs_p, pp, gla_n, ret_n, i)
            ps = _odd_proj(x, mod5, norm_g3, o_w_in, o_w_gate, o_b_gate, l, latent=True, tables=tab_r)
            s0 = (jnp.swapaxes(state_gla[:, i], -1, -2), jnp.swapaxes(state_ret[:, i], -1, -2))
            outs_s = _scan(ps, log_gamma[i], wall, n_seq=DEC_BATCH, t_seq=DEC_SEQ, s0=s0, emit_state=False)
            mix_s = _odd_post(outs_s, ps, gla_n, ret_n, i)
            w_out, w_out_idx = o_w_out, i
        mix = jnp.concatenate([mix_p, mix_s], axis=0)
        if l == DEPTH - 1:
            x, y = _ffn(x, mod5, norm_g3, wg, wu, wd, l, 1, mix=mix, w_out=w_out, w_out_idx=w_out_idx,
                        final_g=final_g.reshape(1, D_MODEL))
        else:
            x = _ffn(x, mod5, norm_g3, wg, wu, wd, l, 1, mix=mix, w_out=w_out, w_out_idx=w_out_idx)

    return (y[:P_TOK].reshape(BATCH, SEQ, D_MODEL), y[P_TOK:].reshape(DEC_BATCH, DEC_SEQ, D_MODEL),
            jnp.stack(new_ckv, axis=1), jnp.stack(new_krope, axis=1),
            jnp.stack(new_dk, axis=1), jnp.stack(new_dv, axis=1),
            jnp.stack(new_sg, axis=1), jnp.stack(new_sr, axis=1))
```

```python
import functools
import math

import numpy as np
import jax
import jax.numpy as jnp
from jax import lax
from jax.experimental import pallas as pl
from jax.experimental.pallas import tpu as pltpu

F32 = jnp.float32
BF16 = jnp.bfloat16

D_MODEL = 1024
BATCH = 16
SEQ = 256
DEPTH = 4
DEC_BATCH = 4
DEC_SEQ = 2048
PAST_LEN = 512
GRID_W = 64
N_EVEN = (DEPTH + 1) // 2
N_ODD = DEPTH // 2
N_MOD = 9
D_FF = 2816
H_A, NOPE_A, ROPE_A, V_A = 8, 64, 32, 64
Q_LORA, KV_LORA = 384, 256
H_B, DH_B = 4, 64
DV_B = 2 * DH_B
H_C, DK_C, DV_C = 4, 64, 128
GATE_RANK = 16
GATE_TAU = 16.0
H_D, DK_D, DV_D = 4, 64, 128
CHUNK = 64
ROPE_BASE = 10000.0
EPS = 1e-6

LANES = 128
SUBLANES = 8
VMEM_LIMIT_BYTES = 56 * 1024 * 1024

P_TOK = BATCH * SEQ
S_TOK = DEC_BATCH * DEC_SEQ
N_TOK = P_TOK + S_TOK
N_GROUPS = SUBLANES

TM = 512
FF_CHUNK = 256
TQ = 256
HEAD_W = LANES
VD_W = 2 * LANES
LOG2E = math.log2(math.e)
EVEN_W = Q_LORA + KV_LORA + HEAD_W + 3 * H_B * 2 * DH_B
ODD_MAIN = 3072
ODD_W = ODD_MAIN + LANES
ODD_OUT = ODD_MAIN + 2 * H_C * DK_C
LEVELS = (32, 16, 8, 4, 2, 1)
N_WALL = len(LEVELS) + 2


def _dot(a, b):
    return jnp.dot(a, b, preferred_element_type=F32)


def _dot_nt(a, b):
    return lax.dot_general(a, b, (((1,), (1,)), ((), ())), preferred_element_type=F32)


def _dot_tn(a, b):
    return lax.dot_general(a, b, (((0,), (0,)), ((), ())), preferred_element_type=F32)


def _silu(x):
    return x * (1.0 / (1.0 + jnp.exp(-x)))


def _rms(x, g):
    return x * lax.rsqrt(jnp.mean(x * x, axis=-1, keepdims=True) + EPS) * g


def _modulate(x, g, shift, scale):
    return _rms(x, g) * (1.0 + scale) + shift


def _group_of_block(i, tm):
    return jnp.where(i < P_TOK // tm, 0, 1 + (i - P_TOK // tm) // (DEC_SEQ // tm))


def _cparams(*sem):
    return pltpu.CompilerParams(dimension_semantics=sem, vmem_limit_bytes=VMEM_LIMIT_BYTES)


def _resident(shape, index_map):
    return pl.BlockSpec(shape, index_map, pipeline_mode=pl.Buffered(1))


def _adaln_kernel(c_ref, w_ref, b_ref, o_ref):
    s = _silu(c_ref[...]).astype(BF16)
    o_ref[...] = _dot(s, w_ref[...].astype(BF16)) + b_ref[...]


def _adaln(cond8, mod_w, mod_b):
    mod_b4 = mod_b.reshape(DEPTH, N_MOD, 1, D_MODEL)
    return pl.pallas_call(
        _adaln_kernel,
        grid=(DEPTH, N_MOD),
        in_specs=[
            pl.BlockSpec((N_GROUPS, D_MODEL), lambda l, j: (0, 0)),
            pl.BlockSpec((None, D_MODEL, D_MODEL), lambda l, j: (l, 0, j)),
            pl.BlockSpec((None, None, 1, D_MODEL), lambda l, j: (l, j, 0, 0)),
        ],
        out_specs=pl.BlockSpec((None, None, N_GROUPS, D_MODEL), lambda l, j: (l, j, 0, 0)),
        out_shape=jax.ShapeDtypeStruct((DEPTH, N_MOD, N_GROUPS, D_MODEL), F32),
        compiler_params=_cparams("arbitrary", "arbitrary"),
        name="adaln",
    )(cond8, mod_w, mod_b4)


def _mod_spec(l, j, grp):
    return pl.BlockSpec((None, None, None, 1, D_MODEL), lambda i: (l, j, grp(i), 0, 0))


def _split_specs(width):
    npb = P_TOK // TM
    return [pl.BlockSpec((TM, width), lambda i: (jnp.minimum(i, npb - 1), 0)),
            pl.BlockSpec((TM, width), lambda i: (jnp.maximum(i - npb, 0), 0))]


def _ffn_kernel(*refs, split_x, has_mix, final):
    it = iter(refs)
    is_ctx = pl.program_id(0) < P_TOK // TM
    pick = lambda p_ref, s_ref: jnp.where(is_ctx, p_ref[...], s_ref[...])
    x = pick(next(it), next(it)) if split_x else next(it)[...]
    if has_mix:
        mix = pick(next(it), next(it))
        wout_ref, gmix_ref = next(it), next(it)
    sh_ref, sc_ref, gt_ref, ng_ref = next(it), next(it), next(it), next(it)
    wg_ref, wu_ref, wd_ref = next(it), next(it), next(it)
    if final:
        fg_ref, yp_ref, ys_ref = next(it), next(it), next(it)
    else:
        o_ref = next(it)

    if has_mix:
        x = x + gmix_ref[...] * _dot(mix, wout_ref[...])
    h = _modulate(x, ng_ref[...], sh_ref[...], sc_ref[...]).astype(BF16)
    acc = jnp.zeros(x.shape, F32)
    for j in range(D_FF // FF_CHUNK):
        cols = slice(j * FF_CHUNK, (j + 1) * FF_CHUNK)
        a = _dot(h, wg_ref[:, cols])
        u = _dot(h, wu_ref[:, cols])
        acc = acc + _dot((_silu(a) * u).astype(BF16), wd_ref[cols, :])
    out = x + 0.5 * gt_ref[...] * acc
    if final:
        y = _rms(out, fg_ref[...])

        @pl.when(is_ctx)
        def _():
            yp_ref[...] = y

        @pl.when(jnp.logical_not(is_ctx))
        def _():
            ys_ref[...] = y
    else:
        o_ref[...] = out


def _ffn(x, mod5, norm_g3, wg, wu, wd, l, k, *, mix=None, w_out=None, w_out_idx=None, final_g=None):
    j = 0 if k == 0 else 6
    grp = functools.partial(_group_of_block, tm=TM)
    row_spec = pl.BlockSpec((TM, D_MODEL), lambda i: (i, 0))
    split_x = isinstance(x, tuple)
    in_specs = _split_specs(D_MODEL) if split_x else [row_spec]
    args = list(x) if split_x else [x]
    if mix is not None:
        in_specs += _split_specs(D_MODEL) + [_resident((None, D_MODEL, D_MODEL), lambda i: (w_out_idx, 0, 0)),
                                             _mod_spec(l, 5, grp)]
        args += [*mix, w_out, mod5]
    in_specs += [_mod_spec(l, j, grp), _mod_spec(l, j + 1, grp), _mod_spec(l, j + 2, grp),
                 pl.BlockSpec((None, 1, D_MODEL), lambda i: (3 * l + (0 if k == 0 else 2), 0, 0)),
                 _resident((None, None, D_MODEL, D_FF), lambda i: (l, k, 0, 0)),
                 _resident((None, None, D_MODEL, D_FF), lambda i: (l, k, 0, 0)),
                 _resident((None, None, D_FF, D_MODEL), lambda i: (l, k, 0, 0))]
    args += [mod5, mod5, mod5, norm_g3, wg, wu, wd]
    if final_g is not None:
        in_specs.append(pl.BlockSpec((1, D_MODEL), lambda i: (0, 0)))
        args.append(final_g)
        out_specs = _split_specs(D_MODEL)
        out_shape = [jax.ShapeDtypeStruct((P_TOK, D_MODEL), F32), jax.ShapeDtypeStruct((S_TOK, D_MODEL), F32)]
    else:
        out_specs = row_spec
        out_shape = jax.ShapeDtypeStruct((N_TOK, D_MODEL), F32)
    return pl.pallas_call(
        functools.partial(_ffn_kernel, split_x=split_x, has_mix=mix is not None, final=final_g is not None),
        grid=(N_TOK // TM,),
        in_specs=in_specs,
        out_specs=out_specs,
        out_shape=out_shape,
        compiler_params=_cparams("arbitrary"),
        name=f"ffn_l{l}_{k}",
    )(*args)


def _rope_tables(subvectors, positions):
    t = next(iter(positions.values())).shape[0]
    cos = np.ones((t, LANES), np.float32)
    sin_m = np.zeros((t, LANES), np.float32)
    sin_p = np.zeros((t, LANES), np.float32)
    for first, n, key in subvectors:
        half = n // 2
        inv = np.float32(ROPE_BASE) ** (-np.arange(half, dtype=np.float32) * np.float32(2.0) / np.float32(n))
        ang = positions[key][:, None] * inv[None, :].astype(np.float32)
        c, s = np.cos(ang).astype(np.float32), np.sin(ang).astype(np.float32)
        cos[:, first:first + n] = np.concatenate([c, c], axis=1)
        sin_m[:, first:first + half] = -s
        sin_p[:, first + half:first + n] = s
    return jnp.asarray(cos), jnp.asarray(sin_m), jnp.asarray(sin_p)


def _tile_lanes(t, width):
    return t if width == LANES else jnp.concatenate([t] * (width // LANES), axis=1)


def _rope(y, cos, sin_m, sin_p, half):
    w = y.shape[1]
    c, sm, sp = (_tile_lanes(t, w) for t in (cos, sin_m, sin_p))
    return y * c + pltpu.roll(y, w - half, 1) * sm + pltpu.roll(y, half, 1) * sp


def _ones_lane(lane, width):
    idx = lax.broadcasted_iota(jnp.int32, (1, width), 1)
    return jnp.where((idx & (LANES - 1)) == lane, 1.0, 0.0)


def _even_proj_kernel(*refs, rope, cache):
    it = iter(refs)
    x_ref, sh_ref, sc_ref, ng_ref = next(it), next(it), next(it), next(it)
    win_ref, qn_ref, wuq_ref, kvn_ref, wk_ref, wv_ref = (next(it) for _ in range(6))
    if rope:
        tq = [next(it)[...] for _ in range(3)]
        td = [next(it)[...] for _ in range(3)]
    q_ref, k_ref, va_ref, qd_ref, kd_ref, vd_ref = (next(it) for _ in range(6))
    if cache:
        ckv_ref, krp_ref, kd32_ref, vd32_ref = (next(it) for _ in range(4))

    h = _modulate(x_ref[...], ng_ref[...], sh_ref[...], sc_ref[...]).astype(BF16)
    p = _dot(h, win_ref[...])
    o_ckv = Q_LORA
    o_krp = o_ckv + KV_LORA
    o_qd = o_krp + HEAD_W
    o_kd = o_qd + H_B * 2 * DH_B
    o_vd = o_kd + H_B * 2 * DH_B
    cqn = _rms(p[:, :Q_LORA], qn_ref[...]).astype(BF16)
    q = _dot(cqn, wuq_ref[...])
    ckvn = _rms(p[:, o_ckv:o_krp], kvn_ref[...])
    ckvb = ckvn.astype(BF16)
    krp = p[:, o_krp:o_qd]
    qd = p[:, o_qd:o_kd]
    kd = p[:, o_kd:o_vd]
    vd = p[:, o_vd:]
    if cache:
        ckv_ref[...] = ckvn
        krp_ref[...] = krp
        kd32_ref[...] = kd
        vd32_ref[...] = vd
    if rope:
        q = _rope(q, *tq, half=ROPE_A // 4)
        krp = _rope(krp, *tq, half=ROPE_A // 4)
        qd = _rope(qd, *td, half=DH_B // 4)
        kd = _rope(kd, *td, half=DH_B // 4)
    q_ref[...] = (q * (LOG2E * (NOPE_A + ROPE_A) ** -0.5)).astype(BF16)
    k_ref[...] = (_dot(ckvb, wk_ref[...]) + _tile_lanes(krp, H_A * HEAD_W)).astype(BF16)
    va_ref[...] = (_dot(ckvb, wv_ref[...]) + _ones_lane(V_A, H_A * HEAD_W)).astype(BF16)
    qd_ref[...] = (qd * (LOG2E * DH_B ** -0.5)).astype(BF16)
    kd_ref[...] = kd.astype(BF16)
    one = jnp.broadcast_to(_ones_lane(0, LANES), (vd.shape[0], LANES))
    vd_ref[...] = jnp.concatenate(
        [piece for h in range(H_B) for piece in (vd[:, h * DV_B:(h + 1) * DV_B], one)], axis=1).astype(BF16)


def _even_proj(x, mod5, norm_g3, w_in, q_norm, w_uq, kv_norm, w_k, w_v, l, *, latent, tables=None):
    i = l // 2
    n_rows = S_TOK if latent else P_TOK
    first = P_TOK // TM if latent else 0
    grp = (lambda b: 1 + b // (DEC_SEQ // TM)) if latent else (lambda b: 0)
    row_spec = lambda w: pl.BlockSpec((TM, w), lambda b: (b, 0))
    const = lambda shape: _resident((None,) + shape, lambda b: (i,) + (0,) * len(shape))
    in_specs = [pl.BlockSpec((TM, D_MODEL), lambda b: (first + b, 0)),
                _mod_spec(l, 3, grp), _mod_spec(l, 4, grp),
                pl.BlockSpec((None, 1, D_MODEL), lambda b: (3 * l + 1, 0, 0)),
                const((D_MODEL, EVEN_W)), const((1, Q_LORA)), const((Q_LORA, H_A * HEAD_W)),
                const((1, KV_LORA)), const((KV_LORA, H_A * HEAD_W)), const((KV_LORA, H_A * HEAD_W))]
    args = [x, mod5, mod5, norm_g3, w_in, q_norm, w_uq, kv_norm, w_k, w_v]
    if latent:
        in_specs += [pl.BlockSpec((TM, LANES), lambda b: (b % (DEC_SEQ // TM), 0))] * 6
        args += list(tables)
    widths = [H_A * HEAD_W, H_A * HEAD_W, H_A * HEAD_W, H_B * DV_B, H_B * DV_B, H_B * VD_W]
    out_specs = [row_spec(w) for w in widths]
    out_shape = [jax.ShapeDtypeStruct((n_rows, w), BF16) for w in widths]
    if not latent:
        cache_w = [KV_LORA, HEAD_W, H_B * DV_B, H_B * DV_B]
        out_specs += [row_spec(w) for w in cache_w]
        out_shape += [jax.ShapeDtypeStruct((n_rows, w), F32) for w in cache_w]
    return pl.pallas_call(
        functools.partial(_even_proj_kernel, rope=latent, cache=not latent),
        grid=(n_rows // TM,),
        in_specs=in_specs,
        out_specs=out_specs,
        out_shape=out_shape,
        compiler_params=_cparams("parallel"),
        name=f"even_proj_l{l}_{'s' if latent else 'p'}",
    )(*args)


def _cache_kv_kernel(ckv_ref, krp_ref, wk_ref, wv_ref, k_ref, va_ref):
    c = ckv_ref[...].astype(BF16)
    k_ref[...] = (_dot(c, wk_ref[...]) + _tile_lanes(krp_ref[...], H_A * HEAD_W)).astype(BF16)
    va_ref[...] = (_dot(c, wv_ref[...]) + _ones_lane(V_A, H_A * HEAD_W)).astype(BF16)


def _cache_kv(ckv, krp, w_k, w_v, i):
    n = DEC_BATCH * PAST_LEN
    return pl.pallas_call(
        _cache_kv_kernel,
        grid=(n // PAST_LEN,),
        in_specs=[pl.BlockSpec((PAST_LEN, KV_LORA), lambda b: (b, 0)),
                  pl.BlockSpec((PAST_LEN, HEAD_W), lambda b: (b, 0)),
                  _resident((None, KV_LORA, H_A * HEAD_W), lambda b: (i, 0, 0)),
                  _resident((None, KV_LORA, H_A * HEAD_W), lambda b: (i, 0, 0))],
        out_specs=[pl.BlockSpec((PAST_LEN, H_A * HEAD_W), lambda b: (b, 0)),
                   pl.BlockSpec((PAST_LEN, H_A * HEAD_W), lambda b: (b, 0))],
        out_shape=[jax.ShapeDtypeStruct((n, H_A * HEAD_W), BF16),
                   jax.ShapeDtypeStruct((n, H_A * HEAD_W), BF16)],
        compiler_params=_cparams("parallel"),
        name=f"cache_kv_{i}",
    )(ckv, krp, w_k, w_v)


def _attend(q, ks, vs):
    s = [_dot_nt(q, k) for k in ks]
    m = functools.reduce(jnp.maximum, [jnp.max(si, axis=-1, keepdims=True) for si in s])
    parts = [_dot(jnp.exp2(si - m).astype(BF16), v) for si, v in zip(s, vs)]
    return functools.reduce(jnp.add, parts)


def _attn_kernel(*refs, lam_init, n_seg):
    q_ref, qd_ref = refs[:2]
    segs = [refs[2 + 4 * g:6 + 4 * g] for g in range(n_seg)]
    lam_ref, sub_ref, o_ref = refs[2 + 4 * n_seg:]
    outs = []
    for h in range(H_A):
        cols = slice(h * HEAD_W, (h + 1) * HEAD_W)
        acc = _attend(q_ref[:, cols], [sg[0][:, cols] for sg in segs], [sg[1][:, cols] for sg in segs])
        outs.append(acc[:, :V_A] * (1.0 / acc[:, V_A:V_A + 1]))
    lp = lam_ref[...]
    lam = (jnp.exp(jnp.sum(lp[0:1] * lp[1:2], axis=-1, keepdims=True))
           - jnp.exp(jnp.sum(lp[2:3] * lp[3:4], axis=-1, keepdims=True)) + lam_init)
    first = lax.broadcasted_iota(jnp.int32, (1, DV_B), 1) < DH_B
    for h in range(H_B):
        cols = slice(h * DV_B, (h + 1) * DV_B)
        vcols = slice(h * VD_W, (h + 1) * VD_W)
        qh = qd_ref[:, cols].astype(F32)
        ks = [sg[2][:, cols] for sg in segs]
        vs = [sg[3][:, vcols] for sg in segs]
        a1 = _attend(jnp.where(first, qh, 0.0).astype(BF16), ks, vs)
        a2 = _attend(jnp.where(first, 0.0, qh).astype(BF16), ks, vs)
        o = a1[:, :DV_B] * (1.0 / a1[:, DV_B:DV_B + 1]) - a2[:, :DV_B] * (lam / a2[:, DV_B:DV_B + 1])
        outs.append(_rms(o, sub_ref[...]) * (1.0 - lam_init))
    o_ref[...] = jnp.concatenate(outs, axis=1).astype(BF16)


def _attention(q, qd, segs, lam_p, subln, i, lam_init, *, n_batch, t_q, name):
    nq = t_q // TQ
    q_spec = lambda w: pl.BlockSpec((TQ, w), lambda b, j: (b * nq + j, 0))
    in_specs = [q_spec(H_A * HEAD_W), q_spec(H_B * DV_B)]
    args = [q, qd]
    for seg in segs:
        for a in seg:
            in_specs.append(pl.BlockSpec((None,) + a.shape[1:], lambda b, j: (b, 0, 0)))
            args.append(a)
    in_specs += [pl.BlockSpec((None, 4, DH_B), lambda b, j: (i, 0, 0)),
                 pl.BlockSpec((None, 1, DV_B), lambda b, j: (i, 0, 0))]
    args += [lam_p, subln]
    return pl.pallas_call(
        functools.partial(_attn_kernel, lam_init=lam_init, n_seg=len(segs)),
        grid=(n_batch, nq),
        in_specs=in_specs,
        out_specs=q_spec(D_MODEL),
        out_shape=jax.ShapeDtypeStruct((n_batch * t_q, D_MODEL), BF16),
        compiler_params=_cparams("parallel", "arbitrary"),
        name=name,
    )(*args)


def _odd_proj_kernel(*refs, rope):
    it = iter(refs)
    x_ref, sh_ref, sc_ref, ng_ref, win_ref, wgate_ref, bgate_ref = (next(it) for _ in range(7))
    if rope:
        tr = [next(it)[...] for _ in range(3)]
    o_ref = next(it)

    h = _modulate(x_ref[...], ng_ref[...], sh_ref[...], sc_ref[...]).astype(BF16)
    p = _dot(h, win_ref[...])
    o_ref[:, 0:256] = p[:, 0:256] * (DK_C ** -0.5)
    o_ref[:, 256:1536] = p[:, 256:1536]
    rq = p[:, 1536:1792]
    rk = p[:, 1792:2048] * (DK_D ** -0.5)
    if rope:
        rq = _rope(rq, *tr, half=DK_D // 2)
        rk = _rope(rk, *tr, half=DK_D // 2)
    o_ref[:, 1536:1792] = rq
    o_ref[:, 1792:2048] = rk
    o_ref[:, 2048:ODD_MAIN] = p[:, 2048:ODD_MAIN]
    logit = _dot(p[:, ODD_MAIN:].astype(BF16), wgate_ref[...]) + bgate_ref[...]
    o_ref[:, ODD_MAIN:] = (jnp.minimum(logit, 0.0) - jnp.log1p(jnp.exp(-jnp.abs(logit)))) * (1.0 / GATE_TAU)


def _odd_proj(x, mod5, norm_g3, w_in, w_gate, b_gate, l, *, latent, tables=None):
    i = l // 2
    n_rows = S_TOK if latent else P_TOK
    first = P_TOK // TM if latent else 0
    grp = (lambda b: 1 + b // (DEC_SEQ // TM)) if latent else (lambda b: 0)
    in_specs = [pl.BlockSpec((TM, D_MODEL), lambda b: (first + b, 0)),
                _mod_spec(l, 3, grp), _mod_spec(l, 4, grp),
                pl.BlockSpec((None, 1, D_MODEL), lambda b: (3 * l + 1, 0, 0)),
                _resident((None, D_MODEL, ODD_W), lambda b: (i, 0, 0)),
                _resident((None, LANES, 2 * H_C * DK_C), lambda b: (i, 0, 0)),
                _resident((None, 1, 2 * H_C * DK_C), lambda b: (i, 0, 0))]
    args = [x, mod5, mod5, norm_g3, w_in, w_gate, b_gate]
    if latent:
        in_specs += [pl.BlockSpec((TM, LANES), lambda b: (b % (DEC_SEQ // TM), 0))] * 3
        args += list(tables)
    return pl.pallas_call(
        functools.partial(_odd_proj_kernel, rope=latent),
        grid=(n_rows // TM,),
        in_specs=in_specs,
        out_specs=pl.BlockSpec((TM, ODD_OUT), lambda b: (b, 0)),
        out_shape=jax.ShapeDtypeStruct((n_rows, ODD_OUT), F32),
        compiler_params=_cparams("parallel"),
        name=f"odd_proj_l{l}_{'s' if latent else 'p'}",
    )(*args)


def _scan_weights():
    c = CHUNK
    t = np.arange(c)
    out = np.zeros((2, N_WALL * c, c), np.float32)
    for d in range(2):
        run = (t[None, :] <= t[:, None]) if d == 0 else (t[None, :] >= t[:, None])
        run = run.astype(np.float32)
        out[d, 0:c] = run
        for li, b in enumerate(LEVELS):
            base = (t // (2 * b)) * (2 * b)
            r = base + (b - 1 if d == 0 else b)
            out[d, (li + 1) * c:(li + 2) * c] = run - run[r]
        out[d, (N_WALL - 1) * c:] = 1.0 - run
    return out


def _scan_kernel(*refs, n_chunks, has_s0, emit_state):
    it = iter(refs)
    lg_ref = next(it)
    ins = [[next(it) for _ in range(7)] for _ in range(2)]
    wall_ref = next(it)
    if has_s0:
        s0g_ref, s0r_ref = next(it), next(it)
    o_refs = [[next(it), next(it)] for _ in range(2)]
    if emit_state:
        sgo_ref, sro_ref = next(it), next(it)
    sg_ref, sr_ref = next(it), next(it)
    c_id = pl.program_id(1)

    @pl.when(c_id == 0)
    def _():
        if has_s0:
            sg_ref[...] = s0g_ref[...]
            sr_ref[...] = s0r_ref[...]
        else:
            sg_ref[...] = jnp.zeros(sg_ref.shape, F32)
            sr_ref[...] = jnp.zeros(sr_ref.shape, F32)

    row = lax.broadcasted_iota(jnp.int32, (CHUNK, 1), 0)
    col = lax.broadcasted_iota(jnp.int32, (1, CHUNK), 1)
    dist = jnp.abs(row - col).astype(F32)
    for d in range(2):
        gq_ref, gk_ref, gv_ref, gg_ref, rq_ref, rk_ref, rv_ref = ins[d]
        causal = (row >= col) if d == 0 else (row <= col)
        q, k, v, g = gq_ref[...], gk_ref[...], gv_ref[...], gg_ref[...]
        g_hi = g.astype(BF16)
        g_lo = (g - g_hi.astype(F32)).astype(BF16)
        wall = wall_ref[d]
        dall = _dot(wall, g_hi) + _dot(wall, g_lo)
        bc = dall[0:CHUNK]
        rem = dall[(N_WALL - 1) * CHUNK:]
        tot = bc[CHUNK - 1:CHUNK] if d == 0 else bc[0:1]
        amat = [None] * H_C
        for h in range(H_C):
            hc = slice(h * DK_C, (h + 1) * DK_C)
            amat[h] = jnp.where(row == col, jnp.sum(q[:, hc] * k[:, hc], axis=-1, keepdims=True), 0.0)
        for li, b in enumerate(LEVELS):
            e = jnp.exp(-jnp.abs(dall[(li + 1) * CHUNK:(li + 2) * CHUNK]))
            later = (row & (2 * b - 1)) >= b
            is_q = later if d == 0 else jnp.logical_not(later)
            ql = jnp.where(is_q, q * e, 0.0).astype(BF16)
            kl = jnp.where(is_q, 0.0, k * e).astype(BF16)
            shift = (2 * b).bit_length() - 1
            same = (row >> shift) == (col >> shift)
            for h in range(H_C):
                hc = slice(h * DK_C, (h + 1) * DK_C)
                amat[h] = amat[h] + jnp.where(same, _dot_nt(ql[:, hc], kl[:, hc]), 0.0)
        qb = (q * jnp.exp(bc)).astype(BF16)
        kb = (k * jnp.exp(rem)).astype(BF16)
        et = jnp.exp(tot)
        outs = []
        for h in range(H_C):
            hc = slice(h * DK_C, (h + 1) * DK_C)
            vh = v[:, h * DV_C:(h + 1) * DV_C].astype(BF16)
            st = sg_ref[d, h]
            outs.append(_dot(amat[h].astype(BF16), vh) + _dot_nt(qb[:, hc], st.astype(BF16)))
            sg_ref[d, h] = st * et[:, hc] + _dot_tn(vh, kb[:, hc])
        o_refs[d][0][...] = jnp.concatenate(outs, axis=1)
        rq, rk, rv = rq_ref[...], rk_ref[...], rv_ref[...]
        cnt = ((row + 1) if d == 0 else (CHUNK - row)).astype(F32)
        outs = []
        for h in range(H_D):
            hc = slice(h * DK_D, (h + 1) * DK_D)
            lg = lg_ref[d, h]
            dec = jnp.where(causal, jnp.exp(lg * dist), 0.0)
            vh = rv[:, h * DV_D:(h + 1) * DV_D].astype(BF16)
            st = sr_ref[d, h]
            a = _dot_nt(rq[:, hc].astype(BF16), rk[:, hc].astype(BF16)) * dec
            qs = (rq[:, hc] * jnp.exp(lg * cnt)).astype(BF16)
            ks = (rk[:, hc] * jnp.exp(lg * (CHUNK - cnt))).astype(BF16)
            outs.append(_dot(a.astype(BF16), vh) + _dot_nt(qs, st.astype(BF16)))
            sr_ref[d, h] = st * jnp.exp(lg * CHUNK) + _dot_tn(vh, ks)
        o_refs[d][1][...] = jnp.concatenate(outs, axis=1)

    if emit_state:
        @pl.when(c_id == n_chunks - 1)
        def _():
            sgo_ref[...] = sg_ref[...]
            sro_ref[...] = sr_ref[...]


def _scan(p, log_gamma, wall, *, n_seq, t_seq, s0=None, emit_state):
    nc = t_seq // CHUNK
    fwd = lambda blk: (lambda b, c: (b * nc + c, blk))
    bwd = lambda blk: (lambda b, c: (b * nc + nc - 1 - c, blk))
    in_specs = [pl.BlockSpec(memory_space=pltpu.SMEM)]
    args = [log_gamma]
    for m in (fwd, bwd):
        d = 0 if m is fwd else 1
        in_specs += [pl.BlockSpec((CHUNK, 256), m(0)), pl.BlockSpec((CHUNK, 256), m(1)),
                     pl.BlockSpec((CHUNK, 512), m(1)), pl.BlockSpec((CHUNK, 256), m(ODD_MAIN // 256 + d)),
                     pl.BlockSpec((CHUNK, 256), m(6)), pl.BlockSpec((CHUNK, 256), m(7)),
                     pl.BlockSpec((CHUNK, 512), m(4))]
        args += [p] * 7
    in_specs.append(_resident((2, N_WALL * CHUNK, CHUNK), lambda b, c: (0, 0, 0)))
    args.append(wall)
    st_shape = (2, H_C, DV_C, DK_C)
    st_spec = pl.BlockSpec((None,) + st_shape, lambda b, c: (b, 0, 0, 0, 0))
    if s0 is not None:
        in_specs += [st_spec, st_spec]
        args += list(s0)
    out_specs = [pl.BlockSpec((CHUNK, 512), fwd(0)), pl.BlockSpec((CHUNK, 512), fwd(0)),
                 pl.BlockSpec((CHUNK, 512), bwd(0)), pl.BlockSpec((CHUNK, 512), bwd(0))]
    out_shape = [jax.ShapeDtypeStruct((n_seq * t_seq, 512), F32)] * 4
    if emit_state:
        out_specs += [st_spec, st_spec]
        out_shape += [jax.ShapeDtypeStruct((n_seq,) + st_shape, F32)] * 2
    return pl.pallas_call(
        functools.partial(_scan_kernel, n_chunks=nc, has_s0=s0 is not None, emit_state=emit_state),
        grid=(n_seq, nc),
        in_specs=in_specs,
        out_specs=out_specs,
        out_shape=out_shape,
        scratch_shapes=[pltpu.VMEM(st_shape, F32), pltpu.VMEM(st_shape, F32)],
        compiler_params=_cparams("parallel", "arbitrary"),
        name=f"scan_{'s' if s0 is not None else 'p'}",
    )(*args)


def _odd_post_kernel(ogf_ref, orf_ref, ogb_ref, orb_ref, gr_ref, rr_ref, gn_ref, rn_ref, o_ref):
    og = ogf_ref[...] + ogb_ref[...]
    orr = orf_ref[...] + orb_ref[...]
    gr, rr = gr_ref[...], rr_ref[...]
    outs = []
    for h in range(H_C):
        hc = slice(h * DV_C, (h + 1) * DV_C)
        outs.append(_rms(og[:, hc], gn_ref[...]) * _silu(gr[:, hc]))
    for h in range(H_D):
        hc = slice(h * DV_D, (h + 1) * DV_D)
        xh = orr[:, hc]
        mu = jnp.mean(xh, axis=-1, keepdims=True)
        xc = xh - mu
        var = jnp.mean(xc * xc, axis=-1, keepdims=True)
        outs.append(xc * lax.rsqrt(var + EPS) * rn_ref[...] * _silu(rr[:, hc]))
    o_ref[...] = jnp.concatenate(outs, axis=1).astype(BF16)


def _odd_post(scan_outs, p, gla_norm, ret_norm, i):
    n = p.shape[0]
    row = lambda blk: pl.BlockSpec((TM, 512), lambda b: (b, blk))
    return pl.pallas_call(
        _odd_post_kernel,
        grid=(n // TM,),
        in_specs=[row(0)] * 4 + [row(2), row(5),
                                 pl.BlockSpec((None, 1, DV_C), lambda b: (i, 0, 0)),
                                 pl.BlockSpec((None, 1, DV_D), lambda b: (i, 0, 0))],
        out_specs=pl.BlockSpec((TM, D_MODEL), lambda b: (b, 0)),
        out_shape=jax.ShapeDtypeStruct((n, D_MODEL), BF16),
        compiler_params=_cparams("parallel"),
        name="odd_post",
    )(*scan_outs, p, p, gla_norm, ret_norm)


def _pad_cols(w, width):
    return jnp.pad(w, [(0, 0)] * (w.ndim - 1) + [(0, width - w.shape[-1])])


def _even_weights(even_w_in, mla_w_uq, mla_w_ukv):
    cq, ckv, kr, qd, kd, vd = jnp.split(even_w_in, np.cumsum([Q_LORA, KV_LORA, ROPE_A, 512, 512]).tolist(), axis=-1)
    zeros = lambda n: jnp.zeros(even_w_in.shape[:-1] + (n,), even_w_in.dtype)
    w_in = jnp.concatenate([cq, ckv, zeros(NOPE_A), kr, zeros(HEAD_W - NOPE_A - ROPE_A), qd, kd, vd], axis=-1)
    uq = mla_w_uq.reshape(N_EVEN, Q_LORA, H_A, NOPE_A + ROPE_A)
    w_uq = _pad_cols(uq, HEAD_W).reshape(N_EVEN, Q_LORA, H_A * HEAD_W)
    ukv = mla_w_ukv.reshape(N_EVEN, KV_LORA, H_A, NOPE_A + V_A)
    w_k = _pad_cols(ukv[..., :NOPE_A], HEAD_W).reshape(N_EVEN, KV_LORA, H_A * HEAD_W)
    w_v = _pad_cols(ukv[..., NOPE_A:], HEAD_W).reshape(N_EVEN, KV_LORA, H_A * HEAD_W)
    return tuple(w.astype(BF16) for w in (w_in, w_uq, w_k, w_v))


def _odd_weights(odd_w_in, gla_w_gate, gla_b_gate):
    cuts = np.cumsum([256, 256, 512, 2 * GATE_RANK, 512, 256, 256, 512]).tolist()
    gq, gk, gv, glr, gr, rq, rk, rv, rr = jnp.split(odd_w_in, cuts, axis=-1)
    w_in = jnp.concatenate([gq, gk, gv, gr, rq, rk, rv, rr, _pad_cols(glr, LANES)], axis=-1)
    w_gate = jnp.zeros((N_ODD, LANES, 2 * H_C * DK_C), F32)
    for d in range(2):
        w_gate = w_gate.at[:, d * GATE_RANK:(d + 1) * GATE_RANK, d * 256:(d + 1) * 256].set(gla_w_gate[:, d])
    b_gate = gla_b_gate.reshape(N_ODD, 1, 2 * H_C * DK_C)
    return w_in.astype(BF16), w_gate.astype(BF16), b_gate


def kernel(x_prompt, x_sample, cache_mla_ckv, cache_mla_krope, cache_diff_k, cache_diff_v, state_gla, state_ret, c, c_ctx, mod_w, mod_b, norm_g, ffn_w_gate, ffn_w_up, ffn_w_down, even_w_in, mla_q_norm, mla_w_uq, mla_kv_norm, mla_w_ukv, diff_lambda, diff_subln, even_w_out, odd_w_in, gla_w_gate, gla_b_gate, gla_norm, ret_decay, ret_norm, odd_w_out, final_g):
    x = (x_prompt.reshape(P_TOK, D_MODEL), x_sample.reshape(S_TOK, D_MODEL))
    cond8 =jnp.concatenate([c_ctx[None], c, jnp.zeros((N_GROUPS - 1 - DEC_BATCH, D_MODEL), F32)], axis=0)
    mod5 = _adaln(cond8, mod_w, mod_b).reshape(DEPTH, N_MOD, N_GROUPS, 1, D_MODEL)
    norm_g3 = norm_g.reshape(DEPTH * 3, 1, D_MODEL)
    wg, wu, wd = ffn_w_gate.astype(BF16), ffn_w_up.astype(BF16), ffn_w_down.astype(BF16)
    e_w_in, e_w_uq, e_w_k, e_w_v = _even_weights(even_w_in, mla_w_uq, mla_w_ukv)
    o_w_in, o_w_gate, o_b_gate = _odd_weights(odd_w_in, gla_w_gate, gla_b_gate)
    e_w_out, o_w_out = even_w_out.astype(BF16), odd_w_out.astype(BF16)
    q_norm = mla_q_norm.reshape(N_EVEN, 1, Q_LORA)
    kv_norm = mla_kv_norm.reshape(N_EVEN, 1, KV_LORA)
    subln = diff_subln.reshape(N_EVEN, 1, DV_B)
    gla_n = gla_norm.reshape(N_ODD, 1, DV_C)
    ret_n = ret_norm.reshape(N_ODD, 1, DV_D)
    log_gamma = jnp.log1p(-jnp.exp2(-ret_decay))
    wall = jnp.asarray(_scan_weights(), BF16)

    tpos = np.arange(DEC_SEQ, dtype=np.float32)
    pos = {"row": np.floor(tpos / GRID_W), "col": tpos % GRID_W, "t": tpos}
    tab_q = _rope_tables([(NOPE_A, ROPE_A // 2, "row"), (NOPE_A + ROPE_A // 2, ROPE_A // 2, "col")], pos)
    tab_d = _rope_tables([(o, DH_B // 2, key) for o, key in ((0, "row"), (32, "col"), (64, "row"), (96, "col"))], pos)
    tab_r = _rope_tables([(0, DK_D, "t"), (DK_D, DK_D, "t")], pos)

    mix = None
    w_out, w_out_idx = None, None
    new_ckv, new_krope, new_dk, new_dv, new_sg, new_sr = [], [], [], [], [], []
    for l in range(DEPTH):
        i = l // 2
        x = _ffn(x, mod5, norm_g3, wg, wu, wd, l, 0)
        if l % 2 == 0:
            lam_init = 0.8 - 0.6 * math.exp(-0.3 * l)
            qp, kp, vap, qdp, kdp, vdp, ckv, krp, kd32, vd32 = _even_proj(
                x, mod5, norm_g3, e_w_in, q_norm, e_w_uq, kv_norm, e_w_k, e_w_v, l, latent=False)
            new_ckv.append(ckv.reshape(BATCH, SEQ, KV_LORA))
            new_krope.append(krp[:, NOPE_A:NOPE_A + ROPE_A].reshape(BATCH, SEQ, ROPE_A))
            new_dk.append(kd32.reshape(BATCH, SEQ, H_B, 2 * DH_B))
            new_dv.append(vd32.reshape(BATCH, SEQ, H_B, DV_B))
            per_seq = lambda n, t: (lambda a: a.reshape(n, t, a.shape[-1]))
            mix_p = _attention(qp, qdp, [tuple(map(per_seq(BATCH, SEQ), (kp, vap, kdp, vdp)))],
                               diff_lambda, subln, i, lam_init, n_batch=BATCH, t_q=SEQ, name=f"attn_l{l}_p")
            qs, ks, vas, qds, kds, vds = _even_proj(
                x, mod5, norm_g3, e_w_in, q_norm, e_w_uq, kv_norm, e_w_k, e_w_v, l, latent=True,
                tables=tab_q + tab_d)
            krp_c = jnp.pad(cache_mla_krope[:, i].reshape(DEC_BATCH * PAST_LEN, ROPE_A),
                            ((0, 0), (NOPE_A, HEAD_W - NOPE_A - ROPE_A)))
            kc, vac = _cache_kv(cache_mla_ckv[:, i].reshape(DEC_BATCH * PAST_LEN, KV_LORA), krp_c, e_w_k, e_w_v, i)
            kdc = cache_diff_k[:, i].reshape(DEC_BATCH, PAST_LEN, H_B * DV_B).astype(BF16)
            vdc = jnp.concatenate([cache_diff_v[:, i], jnp.ones((DEC_BATCH, PAST_LEN, H_B, 1), F32),
                                   jnp.zeros((DEC_BATCH, PAST_LEN, H_B, VD_W - DV_B - 1), F32)], axis=-1)
            vdc = vdc.reshape(DEC_BATCH, PAST_LEN, H_B * VD_W).astype(BF16)
            seg_c = tuple(map(per_seq(DEC_BATCH, PAST_LEN), (kc, vac))) + (kdc, vdc)
            seg_n = tuple(map(per_seq(DEC_BATCH, DEC_SEQ), (ks, vas, kds, vds)))
            mix_s = _attention(qs, qds, [seg_c, seg_n], diff_lambda, subln, i, lam_init,
                               n_batch=DEC_BATCH, t_q=DEC_SEQ, name=f"attn_l{l}_s")
            w_out, w_out_idx = e_w_out, i
        else:
            pp = _odd_proj(x, mod5, norm_g3, o_w_in, o_w_gate, o_b_gate, l, latent=False)
            *outs_p, sg, sr = _scan(pp, log_gamma[i], wall, n_seq=BATCH, t_seq=SEQ, emit_state=True)
            new_sg.append(jnp.swapaxes(sg, -1, -2))
            new_sr.append(jnp.swapaxes(sr, -1, -2))
            mix_p = _odd_post(outs_p, pp, gla_n, ret_n, i)
            ps = _odd_proj(x, mod5, norm_g3, o_w_in, o_w_gate, o_b_gate, l, latent=True, tables=tab_r)
            s0 = (jnp.swapaxes(state_gla[:, i], -1, -2), jnp.swapaxes(state_ret[:, i], -1, -2))
            outs_s = _scan(ps, log_gamma[i], wall, n_seq=DEC_BATCH, t_seq=DEC_SEQ, s0=s0, emit_state=False)
            mix_s = _odd_post(outs_s, ps, gla_n, ret_n, i)
            w_out, w_out_idx = o_w_out, i
        x = _ffn(x, mod5, norm_g3, wg, wu, wd, l, 1, mix=(mix_p, mix_s), w_out=w_out, w_out_idx=w_out_idx,
                 final_g=final_g.reshape(1, D_MODEL) if l == DEPTH - 1 else None)

    y_p, y_s = x
    return (y_p.reshape(BATCH, SEQ, D_MODEL), y_s.reshape(DEC_BATCH, DEC_SEQ, D_MODEL),
            jnp.stack(new_ckv, axis=1), jnp.stack(new_krope, axis=1),
            jnp.stack(new_dk, axis=1), jnp.stack(new_dv, axis=1),
            jnp.stack(new_sg, axis=1), jnp.stack(new_sr, axis=1))
```

```python
import functools
import math

import numpy as np
import jax
import jax.numpy as jnp
from jax import lax
from jax.experimental import pallas as pl
from jax.experimental.pallas import tpu as pltpu

F32 = jnp.float32
BF16 = jnp.bfloat16

D_MODEL = 1024
BATCH = 16
SEQ = 256
DEPTH = 4
DEC_BATCH = 4
DEC_SEQ = 2048
PAST_LEN = 512
GRID_W = 64
N_EVEN = (DEPTH + 1) // 2
N_ODD = DEPTH // 2
N_MOD = 9
D_FF = 2816
H_A, NOPE_A, ROPE_A, V_A = 8, 64, 32, 64
Q_LORA, KV_LORA = 384, 256
H_B, DH_B = 4, 64
DV_B = 2 * DH_B
H_C, DK_C, DV_C = 4, 64, 128
GATE_RANK = 16
GATE_TAU = 16.0
H_D, DK_D, DV_D = 4, 64, 128
CHUNK = 64
ROPE_BASE = 10000.0
EPS = 1e-6

LANES = 128
SUBLANES = 8
VMEM_LIMIT_BYTES = 56 * 1024 * 1024

P_TOK = BATCH * SEQ
S_TOK = DEC_BATCH * DEC_SEQ
N_TOK = P_TOK + S_TOK
N_GROUPS = SUBLANES

TM = 512
FF_CHUNK = 256
TQ = 256
HEAD_W = LANES
VD_W = 2 * LANES
LOG2E = math.log2(math.e)
EVEN_W = Q_LORA + KV_LORA + HEAD_W + 3 * H_B * 2 * DH_B
ODD_MAIN = 3072
ODD_W = ODD_MAIN + LANES
ODD_OUT = ODD_MAIN + 2 * H_C * DK_C
SCAN_SEQS = 2
LEVELS = (32, 16, 8, 4, 2, 1)
N_WALL = len(LEVELS) + 2


def _dot(a, b):
    return jnp.dot(a, b, preferred_element_type=F32)


def _dot_nt(a, b):
    return lax.dot_general(a, b, (((1,), (1,)), ((), ())), preferred_element_type=F32)


def _dot_tn(a, b):
    return lax.dot_general(a, b, (((0,), (0,)), ((), ())), preferred_element_type=F32)


def _silu(x):
    return x * (1.0 / (1.0 + jnp.exp(-x)))


def _rms(x, g):
    return x * lax.rsqrt(jnp.mean(x * x, axis=-1, keepdims=True) + EPS) * g


def _modulate(x, g, shift, scale):
    return _rms(x, g) * (1.0 + scale) + shift


def _group_of_block(i, tm):
    return jnp.where(i < P_TOK // tm, 0, 1 + (i - P_TOK // tm) // (DEC_SEQ // tm))


def _cparams(*sem):
    return pltpu.CompilerParams(dimension_semantics=sem, vmem_limit_bytes=VMEM_LIMIT_BYTES)


def _resident(shape, index_map):
    return pl.BlockSpec(shape, index_map, pipeline_mode=pl.Buffered(1))


def _adaln_kernel(c_ref, w_ref, b_ref, o_ref):
    s = _silu(c_ref[...]).astype(BF16)
    o_ref[...] = _dot(s, w_ref[...].astype(BF16)) + b_ref[...]


def _adaln(cond8, mod_w, mod_b):
    mod_b4 = mod_b.reshape(DEPTH, N_MOD, 1, D_MODEL)
    return pl.pallas_call(
        _adaln_kernel,
        grid=(DEPTH, N_MOD),
        in_specs=[
            pl.BlockSpec((N_GROUPS, D_MODEL), lambda l, j: (0, 0)),
            pl.BlockSpec((None, D_MODEL, D_MODEL), lambda l, j: (l, 0, j)),
            pl.BlockSpec((None, None, 1, D_MODEL), lambda l, j: (l, j, 0, 0)),
        ],
        out_specs=pl.BlockSpec((None, None, N_GROUPS, D_MODEL), lambda l, j: (l, j, 0, 0)),
        out_shape=jax.ShapeDtypeStruct((DEPTH, N_MOD, N_GROUPS, D_MODEL), F32),
        compiler_params=_cparams("arbitrary", "arbitrary"),
        name="adaln",
    )(cond8, mod_w, mod_b4)


def _mod_spec(l, j, grp):
    return pl.BlockSpec((None, None, None, 1, D_MODEL), lambda i: (l, j, grp(i), 0, 0))


def _split_specs(width):
    npb = P_TOK // TM
    return [pl.BlockSpec((TM, width), lambda i: (jnp.minimum(i, npb - 1), 0)),
            pl.BlockSpec((TM, width), lambda i: (jnp.maximum(i - npb, 0), 0))]


def _ffn_kernel(*refs, split_x, has_mix, final):
    it = iter(refs)
    is_ctx = pl.program_id(0) < P_TOK // TM
    pick = lambda p_ref, s_ref: jnp.where(is_ctx, p_ref[...], s_ref[...])
    x = pick(next(it), next(it)) if split_x else next(it)[...]
    if has_mix:
        mix = pick(next(it), next(it))
        wout_ref, gmix_ref = next(it), next(it)
    sh_ref, sc_ref, gt_ref, ng_ref = next(it), next(it), next(it), next(it)
    wg_ref, wu_ref, wd_ref = next(it), next(it), next(it)
    if final:
        fg_ref, yp_ref, ys_ref = next(it), next(it), next(it)
    else:
        o_ref = next(it)

    if has_mix:
        x = x + gmix_ref[...] * _dot(mix, wout_ref[...])
    h = _modulate(x, ng_ref[...], sh_ref[...], sc_ref[...]).astype(BF16)
    acc = jnp.zeros(x.shape, F32)
    for j in range(D_FF // FF_CHUNK):
        cols = slice(j * FF_CHUNK, (j + 1) * FF_CHUNK)
        a = _dot(h, wg_ref[:, cols])
        u = _dot(h, wu_ref[:, cols])
        acc = acc + _dot((_silu(a) * u).astype(BF16), wd_ref[cols, :])
    out = x + 0.5 * gt_ref[...] * acc
    if final:
        y = _rms(out, fg_ref[...])

        @pl.when(is_ctx)
        def _():
            yp_ref[...] = y

        @pl.when(jnp.logical_not(is_ctx))
        def _():
            ys_ref[...] = y
    else:
        o_ref[...] = out


def _ffn(x, mod5, norm_g3, wg, wu, wd, l, k, *, mix=None, w_out=None, w_out_idx=None, final_g=None):
    j = 0 if k == 0 else 6
    grp = functools.partial(_group_of_block, tm=TM)
    row_spec = pl.BlockSpec((TM, D_MODEL), lambda i: (i, 0))
    split_x = isinstance(x, tuple)
    in_specs = _split_specs(D_MODEL) if split_x else [row_spec]
    args = list(x) if split_x else [x]
    if mix is not None:
        in_specs += _split_specs(D_MODEL) + [_resident((None, D_MODEL, D_MODEL), lambda i: (w_out_idx, 0, 0)),
                                             _mod_spec(l, 5, grp)]
        args += [*mix, w_out, mod5]
    in_specs += [_mod_spec(l, j, grp), _mod_spec(l, j + 1, grp), _mod_spec(l, j + 2, grp),
                 pl.BlockSpec((None, 1, D_MODEL), lambda i: (3 * l + (0 if k == 0 else 2), 0, 0)),
                 _resident((None, None, D_MODEL, D_FF), lambda i: (l, k, 0, 0)),
                 _resident((None, None, D_MODEL, D_FF), lambda i: (l, k, 0, 0)),
                 _resident((None, None, D_FF, D_MODEL), lambda i: (l, k, 0, 0))]
    args += [mod5, mod5, mod5, norm_g3, wg, wu, wd]
    if final_g is not None:
        in_specs.append(pl.BlockSpec((1, D_MODEL), lambda i: (0, 0)))
        args.append(final_g)
        out_specs = _split_specs(D_MODEL)
        out_shape = [jax.ShapeDtypeStruct((P_TOK, D_MODEL), F32), jax.ShapeDtypeStruct((S_TOK, D_MODEL), F32)]
    else:
        out_specs = row_spec
        out_shape = jax.ShapeDtypeStruct((N_TOK, D_MODEL), F32)
    return pl.pallas_call(
        functools.partial(_ffn_kernel, split_x=split_x, has_mix=mix is not None, final=final_g is not None),
        grid=(N_TOK // TM,),
        in_specs=in_specs,
        out_specs=out_specs,
        out_shape=out_shape,
        compiler_params=_cparams("arbitrary"),
        name=f"ffn_l{l}_{k}",
    )(*args)


def _rope_tables(subvectors, positions):
    t = next(iter(positions.values())).shape[0]
    cos = np.ones((t, LANES), np.float32)
    sin_m = np.zeros((t, LANES), np.float32)
    sin_p = np.zeros((t, LANES), np.float32)
    for first, n, key in subvectors:
        half = n // 2
        inv = np.float32(ROPE_BASE) ** (-np.arange(half, dtype=np.float32) * np.float32(2.0) / np.float32(n))
        ang = positions[key][:, None] * inv[None, :].astype(np.float32)
        c, s = np.cos(ang).astype(np.float32), np.sin(ang).astype(np.float32)
        cos[:, first:first + n] = np.concatenate([c, c], axis=1)
        sin_m[:, first:first + half] = -s
        sin_p[:, first + half:first + n] = s
    return jnp.asarray(cos), jnp.asarray(sin_m), jnp.asarray(sin_p)


def _tile_lanes(t, width):
    return t if width == LANES else jnp.concatenate([t] * (width // LANES), axis=1)


def _rope(y, cos, sin_m, sin_p, half):
    w = y.shape[1]
    c, sm, sp = (_tile_lanes(t, w) for t in (cos, sin_m, sin_p))
    return y * c + pltpu.roll(y, w - half, 1) * sm + pltpu.roll(y, half, 1) * sp


def _ones_lane(lane, width):
    idx = lax.broadcasted_iota(jnp.int32, (1, width), 1)
    return jnp.where((idx & (LANES - 1)) == lane, 1.0, 0.0)


def _even_proj_kernel(*refs, rope, cache):
    it = iter(refs)
    x_ref, sh_ref, sc_ref, ng_ref = next(it), next(it), next(it), next(it)
    win_ref, qn_ref, wuq_ref, kvn_ref, wk_ref, wv_ref = (next(it) for _ in range(6))
    if rope:
        tq = [next(it)[...] for _ in range(3)]
        td = [next(it)[...] for _ in range(3)]
    q_ref, k_ref, va_ref, qd_ref, kd_ref, vd_ref = (next(it) for _ in range(6))
    if cache:
        ckv_ref, krp_ref, kd32_ref, vd32_ref = (next(it) for _ in range(4))

    h = _modulate(x_ref[...], ng_ref[...], sh_ref[...], sc_ref[...]).astype(BF16)
    p = _dot(h, win_ref[...])
    o_ckv = Q_LORA
    o_krp = o_ckv + KV_LORA
    o_qd = o_krp + HEAD_W
    o_kd = o_qd + H_B * 2 * DH_B
    o_vd = o_kd + H_B * 2 * DH_B
    cqn = _rms(p[:, :Q_LORA], qn_ref[...]).astype(BF16)
    q = _dot(cqn, wuq_ref[...])
    ckvn = _rms(p[:, o_ckv:o_krp], kvn_ref[...])
    ckvb = ckvn.astype(BF16)
    krp = p[:, o_krp:o_qd]
    qd = p[:, o_qd:o_kd]
    kd = p[:, o_kd:o_vd]
    vd = p[:, o_vd:]
    if cache:
        ckv_ref[...] = ckvn
        krp_ref[...] = krp
        kd32_ref[...] = kd
        vd32_ref[...] = vd
    if rope:
        q = _rope(q, *tq, half=ROPE_A // 4)
        krp = _rope(krp, *tq, half=ROPE_A // 4)
        qd = _rope(qd, *td, half=DH_B // 4)
        kd = _rope(kd, *td, half=DH_B // 4)
    q_ref[...] = (q * (LOG2E * (NOPE_A + ROPE_A) ** -0.5)).astype(BF16)
    k_ref[...] = (_dot(ckvb, wk_ref[...]) + _tile_lanes(krp, H_A * HEAD_W)).astype(BF16)
    va_ref[...] = (_dot(ckvb, wv_ref[...]) + _ones_lane(V_A, H_A * HEAD_W)).astype(BF16)
    qd_ref[...] = (qd * (LOG2E * DH_B ** -0.5)).astype(BF16)
    kd_ref[...] = kd.astype(BF16)
    one = jnp.broadcast_to(_ones_lane(0, LANES), (vd.shape[0], LANES))
    vd_ref[...] = jnp.concatenate(
        [piece for h in range(H_B) for piece in (vd[:, h * DV_B:(h + 1) * DV_B], one)], axis=1).astype(BF16)


def _even_proj(x, mod5, norm_g3, w_in, q_norm, w_uq, kv_norm, w_k, w_v, l, *, latent, tables=None):
    i = l // 2
    n_rows = S_TOK if latent else P_TOK
    first = P_TOK // TM if latent else 0
    grp = (lambda b: 1 + b // (DEC_SEQ // TM)) if latent else (lambda b: 0)
    row_spec = lambda w: pl.BlockSpec((TM, w), lambda b: (b, 0))
    const = lambda shape: _resident((None,) + shape, lambda b: (i,) + (0,) * len(shape))
    in_specs = [pl.BlockSpec((TM, D_MODEL), lambda b: (first + b, 0)),
                _mod_spec(l, 3, grp), _mod_spec(l, 4, grp),
                pl.BlockSpec((None, 1, D_MODEL), lambda b: (3 * l + 1, 0, 0)),
                const((D_MODEL, EVEN_W)), const((1, Q_LORA)), const((Q_LORA, H_A * HEAD_W)),
                const((1, KV_LORA)), const((KV_LORA, H_A * HEAD_W)), const((KV_LORA, H_A * HEAD_W))]
    args = [x, mod5, mod5, norm_g3, w_in, q_norm, w_uq, kv_norm, w_k, w_v]
    if latent:
        in_specs += [pl.BlockSpec((TM, LANES), lambda b: (b % (DEC_SEQ // TM), 0))] * 6
        args += list(tables)
    widths = [H_A * HEAD_W, H_A * HEAD_W, H_A * HEAD_W, H_B * DV_B, H_B * DV_B, H_B * VD_W]
    out_specs = [row_spec(w) for w in widths]
    out_shape = [jax.ShapeDtypeStruct((n_rows, w), BF16) for w in widths]
    if not latent:
        cache_w = [KV_LORA, HEAD_W, H_B * DV_B, H_B * DV_B]
        out_specs += [row_spec(w) for w in cache_w]
        out_shape += [jax.ShapeDtypeStruct((n_rows, w), F32) for w in cache_w]
    return pl.pallas_call(
        functools.partial(_even_proj_kernel, rope=latent, cache=not latent),
        grid=(n_rows // TM,),
        in_specs=in_specs,
        out_specs=out_specs,
        out_shape=out_shape,
        compiler_params=_cparams("parallel"),
        name=f"even_proj_l{l}_{'s' if latent else 'p'}",
    )(*args)


def _cache_kv_kernel(ckv_ref, krp_ref, wk_ref, wv_ref, k_ref, va_ref):
    c = ckv_ref[...].astype(BF16)
    k_ref[...] = (_dot(c, wk_ref[...]) + _tile_lanes(krp_ref[...], H_A * HEAD_W)).astype(BF16)
    va_ref[...] = (_dot(c, wv_ref[...]) + _ones_lane(V_A, H_A * HEAD_W)).astype(BF16)


def _cache_kv(ckv, krp, w_k, w_v, i):
    n = DEC_BATCH * PAST_LEN
    return pl.pallas_call(
        _cache_kv_kernel,
        grid=(n // PAST_LEN,),
        in_specs=[pl.BlockSpec((PAST_LEN, KV_LORA), lambda b: (b, 0)),
                  pl.BlockSpec((PAST_LEN, HEAD_W), lambda b: (b, 0)),
                  _resident((None, KV_LORA, H_A * HEAD_W), lambda b: (i, 0, 0)),
                  _resident((None, KV_LORA, H_A * HEAD_W), lambda b: (i, 0, 0))],
        out_specs=[pl.BlockSpec((PAST_LEN, H_A * HEAD_W), lambda b: (b, 0)),
                   pl.BlockSpec((PAST_LEN, H_A * HEAD_W), lambda b: (b, 0))],
        out_shape=[jax.ShapeDtypeStruct((n, H_A * HEAD_W), BF16),
                   jax.ShapeDtypeStruct((n, H_A * HEAD_W), BF16)],
        compiler_params=_cparams("parallel"),
        name=f"cache_kv_{i}",
    )(ckv, krp, w_k, w_v)


def _attend(q, ks, vs):
    s = [_dot_nt(q, k) for k in ks]
    m = functools.reduce(jnp.maximum, [jnp.max(si, axis=-1, keepdims=True) for si in s])
    parts = [_dot(jnp.exp2(si - m).astype(BF16), v) for si, v in zip(s, vs)]
    return functools.reduce(jnp.add, parts)


def _attn_kernel(*refs, lam_init, n_seg):
    q_ref, qd_ref = refs[:2]
    segs = [refs[2 + 4 * g:6 + 4 * g] for g in range(n_seg)]
    lam_ref, sub_ref, o_ref = refs[2 + 4 * n_seg:]
    outs = []
    for h in range(H_A):
        cols = slice(h * HEAD_W, (h + 1) * HEAD_W)
        acc = _attend(q_ref[:, cols], [sg[0][:, cols] for sg in segs], [sg[1][:, cols] for sg in segs])
        outs.append(acc[:, :V_A] * (1.0 / acc[:, V_A:V_A + 1]))
    lp = lam_ref[...]
    lam = (jnp.exp(jnp.sum(lp[0:1] * lp[1:2], axis=-1, keepdims=True))
           - jnp.exp(jnp.sum(lp[2:3] * lp[3:4], axis=-1, keepdims=True)) + lam_init)
    first = lax.broadcasted_iota(jnp.int32, (1, DV_B), 1) < DH_B
    for h in range(H_B):
        cols = slice(h * DV_B, (h + 1) * DV_B)
        vcols = slice(h * VD_W, (h + 1) * VD_W)
        qh = qd_ref[:, cols].astype(F32)
        ks = [sg[2][:, cols] for sg in segs]
        vs = [sg[3][:, vcols] for sg in segs]
        a1 = _attend(jnp.where(first, qh, 0.0).astype(BF16), ks, vs)
        a2 = _attend(jnp.where(first, 0.0, qh).astype(BF16), ks, vs)
        o = a1[:, :DV_B] * (1.0 / a1[:, DV_B:DV_B + 1]) - a2[:, :DV_B] * (lam / a2[:, DV_B:DV_B + 1])
        outs.append(_rms(o, sub_ref[...]) * (1.0 - lam_init))
    o_ref[...] = jnp.concatenate(outs, axis=1).astype(BF16)


def _attention(q, qd, segs, lam_p, subln, i, lam_init, *, n_batch, t_q, name):
    nq = t_q // TQ
    q_spec = lambda w: pl.BlockSpec((TQ, w), lambda b, j: (b * nq + j, 0))
    in_specs = [q_spec(H_A * HEAD_W), q_spec(H_B * DV_B)]
    args = [q, qd]
    for seg in segs:
        for a in seg:
            in_specs.append(pl.BlockSpec((None,) + a.shape[1:], lambda b, j: (b, 0, 0)))
            args.append(a)
    in_specs += [pl.BlockSpec((None, 4, DH_B), lambda b, j: (i, 0, 0)),
                 pl.BlockSpec((None, 1, DV_B), lambda b, j: (i, 0, 0))]
    args += [lam_p, subln]
    return pl.pallas_call(
        functools.partial(_attn_kernel, lam_init=lam_init, n_seg=len(segs)),
        grid=(n_batch, nq),
        in_specs=in_specs,
        out_specs=q_spec(D_MODEL),
        out_shape=jax.ShapeDtypeStruct((n_batch * t_q, D_MODEL), BF16),
        compiler_params=_cparams("parallel", "arbitrary"),
        name=name,
    )(*args)


def _odd_proj_kernel(*refs, rope):
    it = iter(refs)
    x_ref, sh_ref, sc_ref, ng_ref, win_ref, wgate_ref, bgate_ref = (next(it) for _ in range(7))
    if rope:
        tr = [next(it)[...] for _ in range(3)]
    o_ref = next(it)

    h = _modulate(x_ref[...], ng_ref[...], sh_ref[...], sc_ref[...]).astype(BF16)
    p = _dot(h, win_ref[...])
    o_ref[:, 0:256] = p[:, 0:256] * (DK_C ** -0.5)
    o_ref[:, 256:1536] = p[:, 256:1536]
    rq = p[:, 1536:1792]
    rk = p[:, 1792:2048] * (DK_D ** -0.5)
    if rope:
        rq = _rope(rq, *tr, half=DK_D // 2)
        rk = _rope(rk, *tr, half=DK_D // 2)
    o_ref[:, 1536:1792] = rq
    o_ref[:, 1792:2048] = rk
    o_ref[:, 2048:ODD_MAIN] = p[:, 2048:ODD_MAIN]
    logit = _dot(p[:, ODD_MAIN:].astype(BF16), wgate_ref[...]) + bgate_ref[...]
    o_ref[:, ODD_MAIN:] = (jnp.minimum(logit, 0.0) - jnp.log1p(jnp.exp(-jnp.abs(logit)))) * (1.0 / GATE_TAU)


def _odd_proj(x, mod5, norm_g3, w_in, w_gate, b_gate, l, *, latent, tables=None):
    i = l // 2
    n_rows = S_TOK if latent else P_TOK
    first = P_TOK // TM if latent else 0
    grp = (lambda b: 1 + b // (DEC_SEQ // TM)) if latent else (lambda b: 0)
    in_specs = [pl.BlockSpec((TM, D_MODEL), lambda b: (first + b, 0)),
                _mod_spec(l, 3, grp), _mod_spec(l, 4, grp),
                pl.BlockSpec((None, 1, D_MODEL), lambda b: (3 * l + 1, 0, 0)),
                _resident((None, D_MODEL, ODD_W), lambda b: (i, 0, 0)),
                _resident((None, LANES, 2 * H_C * DK_C), lambda b: (i, 0, 0)),
                _resident((None, 1, 2 * H_C * DK_C), lambda b: (i, 0, 0))]
    args = [x, mod5, mod5, norm_g3, w_in, w_gate, b_gate]
    if latent:
        in_specs += [pl.BlockSpec((TM, LANES), lambda b: (b % (DEC_SEQ // TM), 0))] * 3
        args += list(tables)
    return pl.pallas_call(
        functools.partial(_odd_proj_kernel, rope=latent),
        grid=(n_rows // TM,),
        in_specs=in_specs,
        out_specs=pl.BlockSpec((TM, ODD_OUT), lambda b: (b, 0)),
        out_shape=jax.ShapeDtypeStruct((n_rows, ODD_OUT), F32),
        compiler_params=_cparams("parallel"),
        name=f"odd_proj_l{l}_{'s' if latent else 'p'}",
    )(*args)


def _scan_weights():
    c = CHUNK
    t = np.arange(c)
    out = np.zeros((2, N_WALL * c, c), np.float32)
    for d in range(2):
        run = (t[None, :] <= t[:, None]) if d == 0 else (t[None, :] >= t[:, None])
        run = run.astype(np.float32)
        out[d, 0:c] = run
        for li, b in enumerate(LEVELS):
            base = (t // (2 * b)) * (2 * b)
            r = base + (b - 1 if d == 0 else b)
            out[d, (li + 1) * c:(li + 2) * c] = run - run[r]
        out[d, (N_WALL - 1) * c:] = 1.0 - run
    return out


C_SAME = 0
C_QM = C_SAME + len(LEVELS) + 1
C_DIST = C_QM + 3 * 2 * len(LEVELS)
C_CAUSAL = C_DIST + 1
C_CNT = C_CAUSAL + 2
N_CONST = C_CNT + 2


def _scan_consts():
    c = CHUNK
    t = np.arange(c)[:, None]
    s = np.arange(2 * c)[None, :] % c
    head0 = (np.arange(2 * c)[None, :] < c)
    out = np.zeros((N_CONST, c, 2 * c), np.float32)
    for li, b in enumerate(LEVELS):
        out[C_SAME + li] = (t // (2 * b)) == (s // (2 * b))
        later = (t % (2 * b)) >= b
        for d in range(2):
            is_q = later if d == 0 else ~later
            base = C_QM + 3 * (d * len(LEVELS) + li)
            out[base] = is_q
            out[base + 1] = ~is_q & head0
            out[base + 2] = ~is_q & ~head0
    out[C_SAME + len(LEVELS)] = (t == s)
    out[C_DIST] = np.abs(t - s)
    out[C_CAUSAL] = s <= t
    out[C_CAUSAL + 1] = s >= t
    out[C_CNT] = t + 1
    out[C_CNT + 1] = c - t
    return out


def _pair_rhs(kf, m0, m1):
    return jnp.concatenate([kf * m0, kf * m1], axis=0).astype(BF16)


def _pair_values(v):
    left = lax.broadcasted_iota(jnp.int32, (1, v.shape[1]), 1) < v.shape[1] // 2
    return jnp.concatenate([jnp.where(left, v, 0.0), jnp.where(left, 0.0, v)], axis=0).astype(BF16)


def _scan_kernel(*refs, n_chunks, n_seq, has_s0, emit_state):
    it = iter(refs)
    ins = [[next(it) for _ in range(7)] for _ in range(2)]
    wall_ref, cst_ref, bd_ref, lg_ref = next(it), next(it), next(it), next(it)
    if has_s0:
        s0g_ref, s0r_ref = next(it), next(it)
    o_refs = [[next(it), next(it)] for _ in range(2)]
    if emit_state:
        sgo_ref, sro_ref = next(it), next(it)
    sg_ref, sr_ref = next(it), next(it)
    c_id = pl.program_id(1)

    @pl.when(c_id == 0)
    def _():
        if has_s0:
            sg_ref[...] = s0g_ref[...]
            sr_ref[...] = s0r_ref[...]
        else:
            sg_ref[...] = jnp.zeros(sg_ref.shape, F32)
            sr_ref[...] = jnp.zeros(sr_ref.shape, F32)

    n_lv = len(LEVELS)
    lane = lax.broadcasted_iota(jnp.int32, (1, 2 * DK_C), 1)
    head0 = jnp.where(lane < DK_C, 1.0, 0.0)
    head1 = 1.0 - head0
    bd = bd_ref[...]
    for sq in range(n_seq):
        for d in range(2):
            gq_ref, gk_ref, gv_ref, gg_ref, rq_ref, rk_ref, rv_ref = ins[d]
            g = gg_ref[sq]
            g_hi = g.astype(BF16)
            g_lo = (g - g_hi.astype(F32)).astype(BF16)
            wall = wall_ref[d]
            dall = _dot(wall, g_hi) + _dot(wall, g_lo)
            outs = []
            for p in range(H_C // 2):
                ln = slice(p * 2 * DK_C, (p + 1) * 2 * DK_C)
                q, k = gq_ref[sq, :, ln], gk_ref[sq, :, ln]
                a = _dot_nt(q.astype(BF16), _pair_rhs(k, head0, head1)) * cst_ref[C_SAME + n_lv]
                for li in range(n_lv):
                    e = jnp.exp(-jnp.abs(dall[(li + 1) * CHUNK:(li + 2) * CHUNK, ln]))
                    base = C_QM + 3 * (d * n_lv + li)
                    ke = k * e
                    a = a + _dot_nt((q * (e * cst_ref[base])).astype(BF16),
                                    _pair_rhs(ke, cst_ref[base + 1], cst_ref[base + 2])) * cst_ref[C_SAME + li]
                bc = dall[0:CHUNK, ln]
                tot = bc[CHUNK - 1:CHUNK] if d == 0 else bc[0:1]
                qb = (q * jnp.exp(bc)).astype(BF16)
                kb = (k * jnp.exp(dall[(N_WALL - 1) * CHUNK:, ln])).astype(BF16)
                v = gv_ref[sq, :, p * 2 * DV_C:(p + 1) * 2 * DV_C]
                st = sg_ref[sq, d, p]
                outs.append(_dot(a.astype(BF16), _pair_values(v)) + _dot_nt(qb, st.astype(BF16)))
                sg_ref[sq, d, p] = st * jnp.exp(tot) + _dot_tn(v.astype(BF16), kb) * bd
            o_refs[d][0][sq] = jnp.concatenate(outs, axis=1)
            cnt = cst_ref[C_CNT + d]
            outs = []
            for p in range(H_D // 2):
                ln = slice(p * 2 * DK_D, (p + 1) * 2 * DK_D)
                q, k = rq_ref[sq, :, ln], rk_ref[sq, :, ln]
                lg = lg_ref[d, :, ln]
                dec = jnp.exp(lg * cst_ref[C_DIST]) * cst_ref[C_CAUSAL + d]
                a = _dot_nt(q.astype(BF16), _pair_rhs(k, head0, head1)) * dec
                qs = (q * jnp.exp(lg * cnt)).astype(BF16)
                ks = (k * jnp.exp(lg * (CHUNK - cnt))).astype(BF16)
                v = rv_ref[sq, :, p * 2 * DV_D:(p + 1) * 2 * DV_D]
                st = sr_ref[sq, d, p]
                outs.append(_dot(a.astype(BF16), _pair_values(v)) + _dot_nt(qs, st.astype(BF16)))
                sr_ref[sq, d, p] = st * jnp.exp(lg * CHUNK) + _dot_tn(v.astype(BF16), ks) * bd
            o_refs[d][1][sq] = jnp.concatenate(outs, axis=1)

    if emit_state:
        @pl.when(c_id == n_chunks - 1)
        def _():
            sgo_ref[...] = sg_ref[...]
            sro_ref[...] = sr_ref[...]


def _pair_states(s):
    n, _, h, dk, dv = s.shape
    st = jnp.swapaxes(s, -1, -2).reshape(n, 2, h // 2, 2, dv, dk)
    z = jnp.zeros_like(st[:, :, :, 0])
    top = jnp.concatenate([st[:, :, :, 0], z], axis=-1)
    bot = jnp.concatenate([z, st[:, :, :, 1]], axis=-1)
    return jnp.concatenate([top, bot], axis=-2)


def _unpair_states(sp):
    n, _, hp, dv2, dk2 = sp.shape
    dv, dk = dv2 // 2, dk2 // 2
    st = jnp.stack([sp[..., :dv, :dk], sp[..., dv:, dk:]], axis=3).reshape(n, 2, 2 * hp, dv, dk)
    return jnp.swapaxes(st, -1, -2)


def _scan(p, lg_rows, wall, cst, bd, *, n_seq, t_seq, s0=None, emit_state):
    nc = t_seq // CHUNK
    ns = SCAN_SEQS
    p3 = p.reshape(n_seq, t_seq, ODD_OUT)
    fwd = lambda blk: (lambda b, c: (b, c, blk))
    bwd = lambda blk: (lambda b, c: (b, nc - 1 - c, blk))
    in_specs, args = [], []
    for m in (fwd, bwd):
        d = 0 if m is fwd else 1
        in_specs += [pl.BlockSpec((ns, CHUNK, 256), m(0)), pl.BlockSpec((ns, CHUNK, 256), m(1)),
                     pl.BlockSpec((ns, CHUNK, 512), m(1)), pl.BlockSpec((ns, CHUNK, 256), m(ODD_MAIN // 256 + d)),
                     pl.BlockSpec((ns, CHUNK, 256), m(6)), pl.BlockSpec((ns, CHUNK, 256), m(7)),
                     pl.BlockSpec((ns, CHUNK, 512), m(4))]
        args += [p3] * 7
    in_specs += [_resident(wall.shape, lambda b, c: (0, 0, 0)), _resident(cst.shape, lambda b, c: (0, 0, 0)),
                 _resident(bd.shape, lambda b, c: (0, 0)), _resident(lg_rows.shape, lambda b, c: (0, 0, 0))]
    args += [wall, cst, bd, lg_rows]
    st_shape = (ns, 2, H_C // 2, 2 * DV_C, 2 * DK_C)
    st_spec = pl.BlockSpec(st_shape, lambda b, c: (b, 0, 0, 0, 0))
    if s0 is not None:
        in_specs += [st_spec, st_spec]
        args += list(s0)
    out_specs = [pl.BlockSpec((ns, CHUNK, 512), fwd(0)), pl.BlockSpec((ns, CHUNK, 512), fwd(0)),
                 pl.BlockSpec((ns, CHUNK, 512), bwd(0)), pl.BlockSpec((ns, CHUNK, 512), bwd(0))]
    out_shape = [jax.ShapeDtypeStruct((n_seq, t_seq, 512), F32)] * 4
    if emit_state:
        out_specs += [st_spec, st_spec]
        out_shape += [jax.ShapeDtypeStruct((n_seq,) + st_shape[1:], F32)] * 2
    res = pl.pallas_call(
        functools.partial(_scan_kernel, n_chunks=nc, n_seq=ns, has_s0=s0 is not None, emit_state=emit_state),
        grid=(n_seq // ns, nc),
        in_specs=in_specs,
        out_specs=out_specs,
        out_shape=out_shape,
        scratch_shapes=[pltpu.VMEM(st_shape, F32), pltpu.VMEM(st_shape, F32)],
        compiler_params=_cparams("parallel", "arbitrary"),
        name=f"scan_{'s' if s0 is not None else 'p'}",
    )(*args)
    return [r.reshape(n_seq * t_seq, 512) for r in res[:4]] + list(res[4:])


def _odd_post_kernel(ogf_ref, orf_ref, ogb_ref, orb_ref, gr_ref, rr_ref, gn_ref, rn_ref, o_ref):
    og = ogf_ref[...] + ogb_ref[...]
    orr = orf_ref[...] + orb_ref[...]
    gr, rr = gr_ref[...], rr_ref[...]
    outs = []
    for h in range(H_C):
        hc = slice(h * DV_C, (h + 1) * DV_C)
        outs.append(_rms(og[:, hc], gn_ref[...]) * _silu(gr[:, hc]))
    for h in range(H_D):
        hc = slice(h * DV_D, (h + 1) * DV_D)
        xh = orr[:, hc]
        mu = jnp.mean(xh, axis=-1, keepdims=True)
        xc = xh - mu
        var = jnp.mean(xc * xc, axis=-1, keepdims=True)
        outs.append(xc * lax.rsqrt(var + EPS) * rn_ref[...] * _silu(rr[:, hc]))
    o_ref[...] = jnp.concatenate(outs, axis=1).astype(BF16)


def _odd_post(scan_outs, p, gla_norm, ret_norm, i):
    n = p.shape[0]
    row = lambda blk: pl.BlockSpec((TM, 512), lambda b: (b, blk))
    return pl.pallas_call(
        _odd_post_kernel,
        grid=(n // TM,),
        in_specs=[row(0)] * 4 + [row(2), row(5),
                                 pl.BlockSpec((None, 1, DV_C), lambda b: (i, 0, 0)),
                                 pl.BlockSpec((None, 1, DV_D), lambda b: (i, 0, 0))],
        out_specs=pl.BlockSpec((TM, D_MODEL), lambda b: (b, 0)),
        out_shape=jax.ShapeDtypeStruct((n, D_MODEL), BF16),
        compiler_params=_cparams("parallel"),
        name="odd_post",
    )(*scan_outs, p, p, gla_norm, ret_norm)


def _pad_cols(w, width):
    return jnp.pad(w, [(0, 0)] * (w.ndim - 1) + [(0, width - w.shape[-1])])


def _even_weights(even_w_in, mla_w_uq, mla_w_ukv):
    cq, ckv, kr, qd, kd, vd = jnp.split(even_w_in, np.cumsum([Q_LORA, KV_LORA, ROPE_A, 512, 512]).tolist(), axis=-1)
    zeros = lambda n: jnp.zeros(even_w_in.shape[:-1] + (n,), even_w_in.dtype)
    w_in = jnp.concatenate([cq, ckv, zeros(NOPE_A), kr, zeros(HEAD_W - NOPE_A - ROPE_A), qd, kd, vd], axis=-1)
    uq = mla_w_uq.reshape(N_EVEN, Q_LORA, H_A, NOPE_A + ROPE_A)
    w_uq = _pad_cols(uq, HEAD_W).reshape(N_EVEN, Q_LORA, H_A * HEAD_W)
    ukv = mla_w_ukv.reshape(N_EVEN, KV_LORA, H_A, NOPE_A + V_A)
    w_k = _pad_cols(ukv[..., :NOPE_A], HEAD_W).reshape(N_EVEN, KV_LORA, H_A * HEAD_W)
    w_v = _pad_cols(ukv[..., NOPE_A:], HEAD_W).reshape(N_EVEN, KV_LORA, H_A * HEAD_W)
    return tuple(w.astype(BF16) for w in (w_in, w_uq, w_k, w_v))


def _odd_weights(odd_w_in, gla_w_gate, gla_b_gate):
    cuts = np.cumsum([256, 256, 512, 2 * GATE_RANK, 512, 256, 256, 512]).tolist()
    gq, gk, gv, glr, gr, rq, rk, rv, rr = jnp.split(odd_w_in, cuts, axis=-1)
    w_in = jnp.concatenate([gq, gk, gv, gr, rq, rk, rv, rr, _pad_cols(glr, LANES)], axis=-1)
    w_gate = jnp.zeros((N_ODD, LANES, 2 * H_C * DK_C), F32)
    for d in range(2):
        w_gate = w_gate.at[:, d * GATE_RANK:(d + 1) * GATE_RANK, d * 256:(d + 1) * 256].set(gla_w_gate[:, d])
    b_gate = gla_b_gate.reshape(N_ODD, 1, 2 * H_C * DK_C)
    return w_in.astype(BF16), w_gate.astype(BF16), b_gate


def kernel(x_prompt, x_sample, cache_mla_ckv, cache_mla_krope, cache_diff_k, cache_diff_v, state_gla, state_ret, c, c_ctx, mod_w, mod_b, norm_g, ffn_w_gate, ffn_w_up, ffn_w_down, even_w_in, mla_q_norm, mla_w_uq, mla_kv_norm, mla_w_ukv, diff_lambda, diff_subln, even_w_out, odd_w_in, gla_w_gate, gla_b_gate, gla_norm, ret_decay, ret_norm, odd_w_out, final_g):
    x = (x_prompt.reshape(P_TOK, D_MODEL), x_sample.reshape(S_TOK, D_MODEL))
    cond8 =jnp.concatenate([c_ctx[None], c, jnp.zeros((N_GROUPS - 1 - DEC_BATCH, D_MODEL), F32)], axis=0)
    mod5 = _adaln(cond8, mod_w, mod_b).reshape(DEPTH, N_MOD, N_GROUPS, 1, D_MODEL)
    norm_g3 = norm_g.reshape(DEPTH * 3, 1, D_MODEL)
    wg, wu, wd = ffn_w_gate.astype(BF16), ffn_w_up.astype(BF16), ffn_w_down.astype(BF16)
    e_w_in, e_w_uq, e_w_k, e_w_v = _even_weights(even_w_in, mla_w_uq, mla_w_ukv)
    o_w_in, o_w_gate, o_b_gate = _odd_weights(odd_w_in, gla_w_gate, gla_b_gate)
    e_w_out, o_w_out = even_w_out.astype(BF16), odd_w_out.astype(BF16)
    q_norm = mla_q_norm.reshape(N_EVEN, 1, Q_LORA)
    kv_norm = mla_kv_norm.reshape(N_EVEN, 1, KV_LORA)
    subln = diff_subln.reshape(N_EVEN, 1, DV_B)
    gla_n = gla_norm.reshape(N_ODD, 1, DV_C)
    ret_n = ret_norm.reshape(N_ODD, 1, DV_D)
    log_gamma = jnp.log1p(-jnp.exp2(-ret_decay))
    wall = jnp.asarray(_scan_weights(), BF16)
    cst = jnp.asarray(_scan_consts())
    bd = jnp.asarray(np.kron(np.eye(2, dtype=np.float32), np.ones((DV_C, DK_C), np.float32)))

    tpos = np.arange(DEC_SEQ, dtype=np.float32)
    pos = {"row": np.floor(tpos / GRID_W), "col": tpos % GRID_W, "t": tpos}
    tab_q = _rope_tables([(NOPE_A, ROPE_A // 2, "row"), (NOPE_A + ROPE_A // 2, ROPE_A // 2, "col")], pos)
    tab_d = _rope_tables([(o, DH_B // 2, key) for o, key in ((0, "row"), (32, "col"), (64, "row"), (96, "col"))], pos)
    tab_r = _rope_tables([(0, DK_D, "t"), (DK_D, DK_D, "t")], pos)

    mix = None
    w_out, w_out_idx = None, None
    new_ckv, new_krope, new_dk, new_dv, new_sg, new_sr = [], [], [], [], [], []
    for l in range(DEPTH):
        i = l // 2
        x = _ffn(x, mod5, norm_g3, wg, wu, wd, l, 0)
        if l % 2 == 0:
            lam_init = 0.8 - 0.6 * math.exp(-0.3 * l)
            qp, kp, vap, qdp, kdp, vdp, ckv, krp, kd32, vd32 = _even_proj(
                x, mod5, norm_g3, e_w_in, q_norm, e_w_uq, kv_norm, e_w_k, e_w_v, l, latent=False)
            new_ckv.append(ckv.reshape(BATCH, SEQ, KV_LORA))
            new_krope.append(krp[:, NOPE_A:NOPE_A + ROPE_A].reshape(BATCH, SEQ, ROPE_A))
            new_dk.append(kd32.reshape(BATCH, SEQ, H_B, 2 * DH_B))
            new_dv.append(vd32.reshape(BATCH, SEQ, H_B, DV_B))
            per_seq = lambda n, t: (lambda a: a.reshape(n, t, a.shape[-1]))
            mix_p = _attention(qp, qdp, [tuple(map(per_seq(BATCH, SEQ), (kp, vap, kdp, vdp)))],
                               diff_lambda, subln, i, lam_init, n_batch=BATCH, t_q=SEQ, name=f"attn_l{l}_p")
            qs, ks, vas, qds, kds, vds = _even_proj(
                x, mod5, norm_g3, e_w_in, q_norm, e_w_uq, kv_norm, e_w_k, e_w_v, l, latent=True,
                tables=tab_q + tab_d)
            krp_c = jnp.pad(cache_mla_krope[:, i].reshape(DEC_BATCH * PAST_LEN, ROPE_A),
                            ((0, 0), (NOPE_A, HEAD_W - NOPE_A - ROPE_A)))
            kc, vac = _cache_kv(cache_mla_ckv[:, i].reshape(DEC_BATCH * PAST_LEN, KV_LORA), krp_c, e_w_k, e_w_v, i)
            kdc = cache_diff_k[:, i].reshape(DEC_BATCH, PAST_LEN, H_B * DV_B).astype(BF16)
            vdc = jnp.concatenate([cache_diff_v[:, i], jnp.ones((DEC_BATCH, PAST_LEN, H_B, 1), F32),
                                   jnp.zeros((DEC_BATCH, PAST_LEN, H_B, VD_W - DV_B - 1), F32)], axis=-1)
            vdc = vdc.reshape(DEC_BATCH, PAST_LEN, H_B * VD_W).astype(BF16)
            seg_c = tuple(map(per_seq(DEC_BATCH, PAST_LEN), (kc, vac))) + (kdc, vdc)
            seg_n = tuple(map(per_seq(DEC_BATCH, DEC_SEQ), (ks, vas, kds, vds)))
            mix_s = _attention(qs, qds, [seg_c, seg_n], diff_lambda, subln, i, lam_init,
                               n_batch=DEC_BATCH, t_q=DEC_SEQ, name=f"attn_l{l}_s")
            w_out, w_out_idx = e_w_out, i
        else:
            pp = _odd_proj(x, mod5, norm_g3, o_w_in, o_w_gate, o_b_gate, l, latent=False)
            lg_rows = jnp.repeat(log_gamma[i], DK_D, axis=-1).reshape(2, 1, H_D * DK_D)
            *outs_p, sg, sr = _scan(pp, lg_rows, wall, cst, bd, n_seq=BATCH, t_seq=SEQ, emit_state=True)
            new_sg.append(_unpair_states(sg))
            new_sr.append(_unpair_states(sr))
            mix_p = _odd_post(outs_p, pp, gla_n, ret_n, i)
            ps = _odd_proj(x, mod5, norm_g3, o_w_in, o_w_gate, o_b_gate, l, latent=True, tables=tab_r)
            s0 = (_pair_states(state_gla[:, i]), _pair_states(state_ret[:, i]))
            outs_s = _scan(ps, lg_rows, wall, cst, bd, n_seq=DEC_BATCH, t_seq=DEC_SEQ, s0=s0, emit_state=False)
            mix_s = _odd_post(outs_s, ps, gla_n, ret_n, i)
            w_out, w_out_idx = o_w_out, i
        x = _ffn(x, mod5, norm_g3, wg, wu, wd, l, 1, mix=(mix_p, mix_s), w_out=w_out, w_out_idx=w_out_idx,
                 final_g=final_g.reshape(1, D_MODEL) if l == DEPTH - 1 else None)

    y_p, y_s = x
    return (y_p.reshape(BATCH, SEQ, D_MODEL), y_s.reshape(DEC_BATCH, DEC_SEQ, D_MODEL),
            jnp.stack(new_ckv, axis=1), jnp.stack(new_krope, axis=1),
            jnp.stack(new_dk, axis=1), jnp.stack(new_dv, axis=1),
            jnp.stack(new_sg, axis=1), jnp.stack(new_sr, axis=1))
```

```python
import functools
import math

import numpy as np
import jax
import jax.numpy as jnp
from jax import lax
from jax.experimental import pallas as pl
from jax.experimental.pallas import tpu as pltpu

F32 = jnp.float32
BF16 = jnp.bfloat16

D_MODEL = 1024
BATCH = 16
SEQ = 256
DEPTH = 4
DEC_BATCH = 4
DEC_SEQ = 2048
PAST_LEN = 512
GRID_W = 64
N_EVEN = (DEPTH + 1) // 2
N_ODD = DEPTH // 2
N_MOD = 9
D_FF = 2816
H_A, NOPE_A, ROPE_A, V_A = 8, 64, 32, 64
Q_LORA, KV_LORA = 384, 256
H_B, DH_B = 4, 64
DV_B = 2 * DH_B
H_C, DK_C, DV_C = 4, 64, 128
GATE_RANK = 16
GATE_TAU = 16.0
H_D, DK_D, DV_D = 4, 64, 128
CHUNK = 64
ROPE_BASE = 10000.0
EPS = 1e-6

LANES = 128
SUBLANES = 8
VMEM_LIMIT_BYTES = 56 * 1024 * 1024

P_TOK = BATCH * SEQ
S_TOK = DEC_BATCH * DEC_SEQ
N_TOK = P_TOK + S_TOK
N_GROUPS = SUBLANES

TM = 512
FF_CHUNK = 256
TQ = 256
HEAD_W = LANES
LOG2E = math.log2(math.e)
EVEN_W = Q_LORA + KV_LORA + HEAD_W + 3 * H_B * 2 * DH_B
ODD_MAIN = 3072
ODD_W = ODD_MAIN + LANES
ODD_OUT = ODD_MAIN + 2 * H_C * DK_C
SCAN_SEQS = 4
LEVELS = (32, 16, 8, 4, 2, 1)
N_WALL = len(LEVELS) + 2


def _dot(a, b):
    return jnp.dot(a, b, preferred_element_type=F32)


def _dot_nt(a, b):
    return lax.dot_general(a, b, (((1,), (1,)), ((), ())), preferred_element_type=F32)


def _dot_tn(a, b):
    return lax.dot_general(a, b, (((0,), (0,)), ((), ())), preferred_element_type=F32)


def _silu(x):
    return x * (1.0 / (1.0 + jnp.exp(-x)))


def _rms(x, g):
    return x * lax.rsqrt(jnp.mean(x * x, axis=-1, keepdims=True) + EPS) * g


def _modulate(x, g, shift, scale):
    return _rms(x, g) * (1.0 + scale) + shift


def _group_of_block(i, tm):
    return jnp.where(i < P_TOK // tm, 0, 1 + (i - P_TOK // tm) // (DEC_SEQ // tm))


def _cparams(*sem):
    return pltpu.CompilerParams(dimension_semantics=sem, vmem_limit_bytes=VMEM_LIMIT_BYTES)


def _resident(shape, index_map):
    return pl.BlockSpec(shape, index_map, pipeline_mode=pl.Buffered(1))


def _adaln_kernel(c_ref, w_ref, b_ref, o_ref):
    s = _silu(c_ref[...]).astype(BF16)
    o_ref[...] = _dot(s, w_ref[...].astype(BF16)) + b_ref[...]


def _adaln(cond8, mod_w, mod_b):
    mod_b4 = mod_b.reshape(DEPTH, N_MOD, 1, D_MODEL)
    return pl.pallas_call(
        _adaln_kernel,
        grid=(DEPTH, N_MOD),
        in_specs=[
            pl.BlockSpec((N_GROUPS, D_MODEL), lambda l, j: (0, 0)),
            pl.BlockSpec((None, D_MODEL, D_MODEL), lambda l, j: (l, 0, j)),
            pl.BlockSpec((None, None, 1, D_MODEL), lambda l, j: (l, j, 0, 0)),
        ],
        out_specs=pl.BlockSpec((None, None, N_GROUPS, D_MODEL), lambda l, j: (l, j, 0, 0)),
        out_shape=jax.ShapeDtypeStruct((DEPTH, N_MOD, N_GROUPS, D_MODEL), F32),
        compiler_params=_cparams("arbitrary", "arbitrary"),
        name="adaln",
    )(cond8, mod_w, mod_b4)


def _mod_spec(l, j, grp):
    return pl.BlockSpec((None, None, None, 1, D_MODEL), lambda i: (l, j, grp(i), 0, 0))


def _split_specs(width):
    npb = P_TOK // TM
    return [pl.BlockSpec((TM, width), lambda i: (jnp.minimum(i, npb - 1), 0)),
            pl.BlockSpec((TM, width), lambda i: (jnp.maximum(i - npb, 0), 0))]


def _ffn_kernel(*refs, split_x, has_mix, final):
    it = iter(refs)
    is_ctx = pl.program_id(0) < P_TOK // TM
    pick = lambda p_ref, s_ref: jnp.where(is_ctx, p_ref[...], s_ref[...])
    x = pick(next(it), next(it)) if split_x else next(it)[...]
    if has_mix:
        mix = pick(next(it), next(it))
        wout_ref, gmix_ref = next(it), next(it)
    sh_ref, sc_ref, gt_ref, ng_ref = next(it), next(it), next(it), next(it)
    wg_ref, wu_ref, wd_ref = next(it), next(it), next(it)
    if final:
        fg_ref, yp_ref, ys_ref = next(it), next(it), next(it)
    else:
        o_ref = next(it)

    if has_mix:
        x = x + gmix_ref[...] * _dot(mix, wout_ref[...])
    h = _modulate(x, ng_ref[...], sh_ref[...], sc_ref[...]).astype(BF16)
    acc = jnp.zeros(x.shape, F32)
    for j in range(D_FF // FF_CHUNK):
        cols = slice(j * FF_CHUNK, (j + 1) * FF_CHUNK)
        a = _dot(h, wg_ref[:, cols])
        u = _dot(h, wu_ref[:, cols])
        acc = acc + _dot((_silu(a) * u).astype(BF16), wd_ref[cols, :])
    out = x + 0.5 * gt_ref[...] * acc
    if final:
        y = _rms(out, fg_ref[...])

        @pl.when(is_ctx)
        def _():
            yp_ref[...] = y

        @pl.when(jnp.logical_not(is_ctx))
        def _():
            ys_ref[...] = y
    else:
        o_ref[...] = out


def _ffn(x, mod5, norm_g3, wg, wu, wd, l, k, *, mix=None, w_out=None, w_out_idx=None, final_g=None):
    j = 0 if k == 0 else 6
    grp = functools.partial(_group_of_block, tm=TM)
    row_spec = pl.BlockSpec((TM, D_MODEL), lambda i: (i, 0))
    split_x = isinstance(x, tuple)
    in_specs = _split_specs(D_MODEL) if split_x else [row_spec]
    args = list(x) if split_x else [x]
    if mix is not None:
        in_specs += _split_specs(D_MODEL) + [_resident((None, D_MODEL, D_MODEL), lambda i: (w_out_idx, 0, 0)),
                                             _mod_spec(l, 5, grp)]
        args += [*mix, w_out, mod5]
    in_specs += [_mod_spec(l, j, grp), _mod_spec(l, j + 1, grp), _mod_spec(l, j + 2, grp),
                 pl.BlockSpec((None, 1, D_MODEL), lambda i: (3 * l + (0 if k == 0 else 2), 0, 0)),
                 _resident((None, None, D_MODEL, D_FF), lambda i: (l, k, 0, 0)),
                 _resident((None, None, D_MODEL, D_FF), lambda i: (l, k, 0, 0)),
                 _resident((None, None, D_FF, D_MODEL), lambda i: (l, k, 0, 0))]
    args += [mod5, mod5, mod5, norm_g3, wg, wu, wd]
    if final_g is not None:
        in_specs.append(pl.BlockSpec((1, D_MODEL), lambda i: (0, 0)))
        args.append(final_g)
        out_specs = _split_specs(D_MODEL)
        out_shape = [jax.ShapeDtypeStruct((P_TOK, D_MODEL), F32), jax.ShapeDtypeStruct((S_TOK, D_MODEL), F32)]
    else:
        out_specs = row_spec
        out_shape = jax.ShapeDtypeStruct((N_TOK, D_MODEL), F32)
    return pl.pallas_call(
        functools.partial(_ffn_kernel, split_x=split_x, has_mix=mix is not None, final=final_g is not None),
        grid=(N_TOK // TM,),
        in_specs=in_specs,
        out_specs=out_specs,
        out_shape=out_shape,
        compiler_params=_cparams("arbitrary"),
        name=f"ffn_l{l}_{k}",
    )(*args)


def _rope_tables(subvectors, positions):
    t = next(iter(positions.values())).shape[0]
    cos = np.ones((t, LANES), np.float32)
    sin_m = np.zeros((t, LANES), np.float32)
    sin_p = np.zeros((t, LANES), np.float32)
    for first, n, key in subvectors:
        half = n // 2
        inv = np.float32(ROPE_BASE) ** (-np.arange(half, dtype=np.float32) * np.float32(2.0) / np.float32(n))
        ang = positions[key][:, None] * inv[None, :].astype(np.float32)
        c, s = np.cos(ang).astype(np.float32), np.sin(ang).astype(np.float32)
        cos[:, first:first + n] = np.concatenate([c, c], axis=1)
        sin_m[:, first:first + half] = -s
        sin_p[:, first + half:first + n] = s
    return jnp.asarray(cos), jnp.asarray(sin_m), jnp.asarray(sin_p)


def _tile_lanes(t, width):
    return t if width == LANES else jnp.concatenate([t] * (width // LANES), axis=1)


def _rope(y, cos, sin_m, sin_p, half):
    w = y.shape[1]
    c, sm, sp = (_tile_lanes(t, w) for t in (cos, sin_m, sin_p))
    return y * c + pltpu.roll(y, w - half, 1) * sm + pltpu.roll(y, half, 1) * sp


def _ones_lane(lane, width):
    idx = lax.broadcasted_iota(jnp.int32, (1, width), 1)
    return jnp.where((idx & (LANES - 1)) == lane, 1.0, 0.0)


def _even_proj_kernel(*refs, rope, cache):
    it = iter(refs)
    x_ref, sh_ref, sc_ref, ng_ref = next(it), next(it), next(it), next(it)
    win_ref, qn_ref, wuq_ref, kvn_ref, wk_ref, wv_ref = (next(it) for _ in range(6))
    if rope:
        tq = [next(it)[...] for _ in range(3)]
        td = [next(it)[...] for _ in range(3)]
    q_ref, k_ref, va_ref, qd_ref, kd_ref, vd_ref = (next(it) for _ in range(6))
    if cache:
        ckv_ref, krp_ref, kd32_ref, vd32_ref = (next(it) for _ in range(4))

    h = _modulate(x_ref[...], ng_ref[...], sh_ref[...], sc_ref[...]).astype(BF16)
    p = _dot(h, win_ref[...])
    o_ckv = Q_LORA
    o_krp = o_ckv + KV_LORA
    o_qd = o_krp + HEAD_W
    o_kd = o_qd + H_B * 2 * DH_B
    o_vd = o_kd + H_B * 2 * DH_B
    cqn = _rms(p[:, :Q_LORA], qn_ref[...]).astype(BF16)
    q = _dot(cqn, wuq_ref[...])
    ckvn = _rms(p[:, o_ckv:o_krp], kvn_ref[...])
    ckvb = ckvn.astype(BF16)
    krp = p[:, o_krp:o_qd]
    qd = p[:, o_qd:o_kd]
    kd = p[:, o_kd:o_vd]
    vd = p[:, o_vd:]
    if cache:
        ckv_ref[...] = ckvn
        krp_ref[...] = krp
        kd32_ref[...] = kd
        vd32_ref[...] = vd
    if rope:
        q = _rope(q, *tq, half=ROPE_A // 4)
        krp = _rope(krp, *tq, half=ROPE_A // 4)
        qd = _rope(qd, *td, half=DH_B // 4)
        kd = _rope(kd, *td, half=DH_B // 4)
    q_ref[...] = (q * (LOG2E * (NOPE_A + ROPE_A) ** -0.5)).astype(BF16)
    k_ref[...] = (_dot(ckvb, wk_ref[...]) + _tile_lanes(krp, H_A * HEAD_W)).astype(BF16)
    va = _dot(ckvb, wv_ref[...])
    if rope:
        va = va + _ones_lane(V_A, H_A * HEAD_W)
    va_ref[...] = va.astype(BF16)
    qd_ref[...] = (qd * (LOG2E * DH_B ** -0.5)).astype(BF16)
    kd_ref[...] = kd.astype(BF16)
    vd_ref[...] = vd.astype(BF16)


def _even_proj(x, mod5, norm_g3, w_in, q_norm, w_uq, kv_norm, w_k, w_v, l, *, latent, tables=None):
    i = l // 2
    n_rows = S_TOK if latent else P_TOK
    first = P_TOK // TM if latent else 0
    grp = (lambda b: 1 + b // (DEC_SEQ // TM)) if latent else (lambda b: 0)
    row_spec = lambda w: pl.BlockSpec((TM, w), lambda b: (b, 0))
    const = lambda shape: _resident((None,) + shape, lambda b: (i,) + (0,) * len(shape))
    in_specs = [pl.BlockSpec((TM, D_MODEL), lambda b: (first + b, 0)),
                _mod_spec(l, 3, grp), _mod_spec(l, 4, grp),
                pl.BlockSpec((None, 1, D_MODEL), lambda b: (3 * l + 1, 0, 0)),
                const((D_MODEL, EVEN_W)), const((1, Q_LORA)), const((Q_LORA, H_A * HEAD_W)),
                const((1, KV_LORA)), const((KV_LORA, H_A * HEAD_W)), const((KV_LORA, w_v.shape[-1]))]
    args = [x, mod5, mod5, norm_g3, w_in, q_norm, w_uq, kv_norm, w_k, w_v]
    if latent:
        in_specs += [pl.BlockSpec((TM, LANES), lambda b: (b % (DEC_SEQ // TM), 0))] * 6
        args += list(tables)
    widths = [H_A * HEAD_W, H_A * HEAD_W, w_v.shape[-1], H_B * DV_B, H_B * DV_B, H_B * DV_B]
    out_specs = [row_spec(w) for w in widths]
    out_shape = [jax.ShapeDtypeStruct((n_rows, w), BF16) for w in widths]
    if not latent:
        cache_w = [KV_LORA, HEAD_W, H_B * DV_B, H_B * DV_B]
        out_specs += [row_spec(w) for w in cache_w]
        out_shape += [jax.ShapeDtypeStruct((n_rows, w), F32) for w in cache_w]
    return pl.pallas_call(
        functools.partial(_even_proj_kernel, rope=latent, cache=not latent),
        grid=(n_rows // TM,),
        in_specs=in_specs,
        out_specs=out_specs,
        out_shape=out_shape,
        compiler_params=_cparams("parallel"),
        name=f"even_proj_l{l}_{'s' if latent else 'p'}",
    )(*args)


def _cache_kv_kernel(ckv_ref, krp_ref, wk_ref, wv_ref, k_ref, va_ref):
    c = ckv_ref[...].astype(BF16)
    k_ref[...] = (_dot(c, wk_ref[...]) + _tile_lanes(krp_ref[...], H_A * HEAD_W)).astype(BF16)
    va_ref[...] = (_dot(c, wv_ref[...]) + _ones_lane(V_A, H_A * HEAD_W)).astype(BF16)


def _cache_kv(ckv, krp, w_k, w_v, i):
    n = DEC_BATCH * PAST_LEN
    return pl.pallas_call(
        _cache_kv_kernel,
        grid=(n // PAST_LEN,),
        in_specs=[pl.BlockSpec((PAST_LEN, KV_LORA), lambda b: (b, 0)),
                  pl.BlockSpec((PAST_LEN, HEAD_W), lambda b: (b, 0)),
                  _resident((None, KV_LORA, H_A * HEAD_W), lambda b: (i, 0, 0)),
                  _resident((None, KV_LORA, H_A * HEAD_W), lambda b: (i, 0, 0))],
        out_specs=[pl.BlockSpec((PAST_LEN, H_A * HEAD_W), lambda b: (b, 0)),
                   pl.BlockSpec((PAST_LEN, H_A * HEAD_W), lambda b: (b, 0))],
        out_shape=[jax.ShapeDtypeStruct((n, H_A * HEAD_W), BF16),
                   jax.ShapeDtypeStruct((n, H_A * HEAD_W), BF16)],
        compiler_params=_cparams("parallel"),
        name=f"cache_kv_{i}",
    )(ckv, krp, w_k, w_v)


def _exp2_parts(s):
    m = functools.reduce(jnp.maximum, [jnp.max(si, axis=-1, keepdims=True) for si in s])
    return [jnp.exp2(si - m) for si in s]


def _rowsum(ps):
    return functools.reduce(jnp.add, [jnp.sum(p, axis=-1, keepdims=True) for p in ps])


def _pv(ps, vs):
    return functools.reduce(jnp.add, [_dot(p.astype(BF16), v) for p, v in zip(ps, vs)])


def _attn_kernel(*refs, lam_init, n_seg, ones_col):
    q_ref, qd_ref = refs[:2]
    segs = [refs[2 + 4 * g:6 + 4 * g] for g in range(n_seg)]
    lam_ref, sub_ref, o_ref = refs[2 + 4 * n_seg:]
    lp = lam_ref[...]
    lam = (jnp.exp(jnp.sum(lp[0:1] * lp[1:2], axis=-1, keepdims=True))
           - jnp.exp(jnp.sum(lp[2:3] * lp[3:4], axis=-1, keepdims=True)) + lam_init)
    first = lax.broadcasted_iota(jnp.int32, (1, DV_B), 1) < DH_B
    va_w = HEAD_W if ones_col else V_A

    def scores(task):
        kind, h = task
        if kind == "mla":
            cols = slice(h * HEAD_W, (h + 1) * HEAD_W)
            return ([_dot_nt(q_ref[:, cols], sg[0][:, cols]) for sg in segs],)
        cols = slice(h * DV_B, (h + 1) * DV_B)
        qh = qd_ref[:, cols].astype(F32)
        q1 = jnp.where(first, qh, 0.0).astype(BF16)
        q2 = jnp.where(first, 0.0, qh).astype(BF16)
        return tuple([_dot_nt(qx, sg[2][:, cols]) for sg in segs] for qx in (q1, q2))

    tasks = [("mla", h) for h in range(H_A)] + [("diff", h) for h in range(H_B)]
    outs = []
    ahead = scores(tasks[0])
    for t, (kind, h) in enumerate(tasks):
        cur = ahead
        if t + 1 < len(tasks):
            ahead = scores(tasks[t + 1])
        if kind == "mla":
            p = _exp2_parts(cur[0])
            acc = _pv(p, [sg[1][:, h * va_w:(h + 1) * va_w] for sg in segs])
            inv = 1.0 / (acc[:, V_A:V_A + 1] if ones_col else _rowsum(p))
            outs.append(acc[:, :V_A] * inv)
        else:
            p1, p2 = _exp2_parts(cur[0]), _exp2_parts(cur[1])
            w1, w2 = 1.0 / _rowsum(p1), lam / _rowsum(p2)
            o = _pv([a * w1 - b * w2 for a, b in zip(p1, p2)],
                    [sg[3][:, h * DV_B:(h + 1) * DV_B] for sg in segs])
            outs.append(_rms(o, sub_ref[...]) * (1.0 - lam_init))
    o_ref[...] = jnp.concatenate(outs, axis=1).astype(BF16)


def _attention(q, qd, segs, lam_p, subln, i, lam_init, *, n_batch, t_q, name):
    nq = t_q // TQ
    q_spec = lambda w: pl.BlockSpec((TQ, w), lambda b, j: (b * nq + j, 0))
    in_specs = [q_spec(H_A * HEAD_W), q_spec(H_B * DV_B)]
    args = [q, qd]
    for seg in segs:
        t_seg = seg[0].shape[0] // n_batch
        for a in seg:
            in_specs.append(pl.BlockSpec((t_seg, a.shape[1]), lambda b, j: (b, 0)))
            args.append(a)
    in_specs += [pl.BlockSpec((None, 4, DH_B), lambda b, j: (i, 0, 0)),
                 pl.BlockSpec((None, 1, DV_B), lambda b, j: (i, 0, 0))]
    args += [lam_p, subln]
    ones_col = segs[0][1].shape[1] == H_A * HEAD_W
    return pl.pallas_call(
        functools.partial(_attn_kernel, lam_init=lam_init, n_seg=len(segs), ones_col=ones_col),
        grid=(n_batch, nq),
        in_specs=in_specs,
        out_specs=q_spec(D_MODEL),
        out_shape=jax.ShapeDtypeStruct((n_batch * t_q, D_MODEL), BF16),
        compiler_params=_cparams("parallel", "arbitrary"),
        name=name,
    )(*args)


def _odd_proj_kernel(*refs, rope):
    it = iter(refs)
    x_ref, sh_ref, sc_ref, ng_ref, win_ref, wgate_ref, bgate_ref = (next(it) for _ in range(7))
    if rope:
        tr = [next(it)[...] for _ in range(3)]
    o_ref = next(it)

    h = _modulate(x_ref[...], ng_ref[...], sh_ref[...], sc_ref[...]).astype(BF16)
    p = _dot(h, win_ref[...])
    o_ref[:, 0:256] = p[:, 0:256] * (DK_C ** -0.5)
    o_ref[:, 256:1536] = p[:, 256:1536]
    rq = p[:, 1536:1792]
    rk = p[:, 1792:2048] * (DK_D ** -0.5)
    if rope:
        rq = _rope(rq, *tr, half=DK_D // 2)
        rk = _rope(rk, *tr, half=DK_D // 2)
    o_ref[:, 1536:1792] = rq
    o_ref[:, 1792:2048] = rk
    o_ref[:, 2048:ODD_MAIN] = p[:, 2048:ODD_MAIN]
    logit = _dot(p[:, ODD_MAIN:].astype(BF16), wgate_ref[...]) + bgate_ref[...]
    o_ref[:, ODD_MAIN:] = (jnp.minimum(logit, 0.0) - jnp.log1p(jnp.exp(-jnp.abs(logit)))) * (1.0 / GATE_TAU)


def _odd_proj(x, mod5, norm_g3, w_in, w_gate, b_gate, l, *, latent, tables=None):
    i = l // 2
    n_rows = S_TOK if latent else P_TOK
    first = P_TOK // TM if latent else 0
    grp = (lambda b: 1 + b // (DEC_SEQ // TM)) if latent else (lambda b: 0)
    in_specs = [pl.BlockSpec((TM, D_MODEL), lambda b: (first + b, 0)),
                _mod_spec(l, 3, grp), _mod_spec(l, 4, grp),
                pl.BlockSpec((None, 1, D_MODEL), lambda b: (3 * l + 1, 0, 0)),
                _resident((None, D_MODEL, ODD_W), lambda b: (i, 0, 0)),
                _resident((None, LANES, 2 * H_C * DK_C), lambda b: (i, 0, 0)),
                _resident((None, 1, 2 * H_C * DK_C), lambda b: (i, 0, 0))]
    args = [x, mod5, mod5, norm_g3, w_in, w_gate, b_gate]
    if latent:
        in_specs += [pl.BlockSpec((TM, LANES), lambda b: (b % (DEC_SEQ // TM), 0))] * 3
        args += list(tables)
    return pl.pallas_call(
        functools.partial(_odd_proj_kernel, rope=latent),
        grid=(n_rows // TM,),
        in_specs=in_specs,
        out_specs=pl.BlockSpec((TM, ODD_OUT), lambda b: (b, 0)),
        out_shape=jax.ShapeDtypeStruct((n_rows, ODD_OUT), F32),
        compiler_params=_cparams("parallel"),
        name=f"odd_proj_l{l}_{'s' if latent else 'p'}",
    )(*args)


def _scan_weights():
    c = CHUNK
    t = np.arange(c)
    out = np.zeros((2, N_WALL * c, c), np.float32)
    for d in range(2):
        run = (t[None, :] <= t[:, None]) if d == 0 else (t[None, :] >= t[:, None])
        run = run.astype(np.float32)
        out[d, 0:c] = run
        for li, b in enumerate(LEVELS):
            base = (t // (2 * b)) * (2 * b)
            r = base + (b - 1 if d == 0 else b)
            out[d, (li + 1) * c:(li + 2) * c] = run - run[r]
        out[d, (N_WALL - 1) * c:] = 1.0 - run
    return out


C_SAME = 0
C_QM = C_SAME + len(LEVELS) + 1
C_DIST = C_QM + 3 * 2 * len(LEVELS)
C_CAUSAL = C_DIST + 1
C_CNT = C_CAUSAL + 2
N_CONST = C_CNT + 2


def _scan_consts():
    c = CHUNK
    t = np.arange(c)[:, None]
    s = np.arange(2 * c)[None, :] % c
    head0 = (np.arange(2 * c)[None, :] < c)
    out = np.zeros((N_CONST, c, 2 * c), np.float32)
    for li, b in enumerate(LEVELS):
        out[C_SAME + li] = (t // (2 * b)) == (s // (2 * b))
        later = (t % (2 * b)) >= b
        for d in range(2):
            is_q = later if d == 0 else ~later
            base = C_QM + 3 * (d * len(LEVELS) + li)
            out[base] = is_q
            out[base + 1] = ~is_q & head0
            out[base + 2] = ~is_q & ~head0
    out[C_SAME + len(LEVELS)] = (t == s)
    out[C_DIST] = np.abs(t - s)
    out[C_CAUSAL] = s <= t
    out[C_CAUSAL + 1] = s >= t
    out[C_CNT] = t + 1
    out[C_CNT + 1] = c - t
    return out


def _pair_rhs(kf, m0, m1):
    return jnp.concatenate([kf * m0, kf * m1], axis=0).astype(BF16)


def _pair_values(v):
    left = lax.broadcasted_iota(jnp.int32, (1, v.shape[1]), 1) < v.shape[1] // 2
    return jnp.concatenate([jnp.where(left, v, 0.0), jnp.where(left, 0.0, v)], axis=0).astype(BF16)


def _scan_kernel(*refs, n_chunks, n_seq, has_s0, emit_state):
    it = iter(refs)
    ins = [[next(it) for _ in range(7)] for _ in range(2)]
    wall_ref, cst_ref, bd_ref, lg_ref = next(it), next(it), next(it), next(it)
    if has_s0:
        s0g_ref, s0r_ref = next(it), next(it)
    o_refs = [[next(it), next(it)] for _ in range(2)]
    if emit_state:
        sgo_ref, sro_ref = next(it), next(it)
    sg_ref, sr_ref = next(it), next(it)
    c_id = pl.program_id(1)

    @pl.when(c_id == 0)
    def _():
        if has_s0:
            sg_ref[...] = s0g_ref[...]
            sr_ref[...] = s0r_ref[...]
        else:
            sg_ref[...] = jnp.zeros(sg_ref.shape, F32)
            sr_ref[...] = jnp.zeros(sr_ref.shape, F32)

    n_lv = len(LEVELS)
    lane = lax.broadcasted_iota(jnp.int32, (1, 2 * DK_C), 1)
    head0 = jnp.where(lane < DK_C, 1.0, 0.0)
    head1 = 1.0 - head0
    bd = bd_ref[...]
    for sq in range(n_seq):
        for d in range(2):
            gq_ref, gk_ref, gv_ref, gg_ref, rq_ref, rk_ref, rv_ref = ins[d]
            g = gg_ref[sq]
            g_hi = g.astype(BF16)
            g_lo = (g - g_hi.astype(F32)).astype(BF16)
            dall = _dot(wall_ref[d], jnp.concatenate([g_hi, g_lo], axis=0))
            outs = []
            for p in range(H_C // 2):
                ln = slice(p * 2 * DK_C, (p + 1) * 2 * DK_C)
                q, k = gq_ref[sq, :, ln], gk_ref[sq, :, ln]
                a = _dot_nt(q.astype(BF16), _pair_rhs(k, head0, head1)) * cst_ref[C_SAME + n_lv]
                for li in range(n_lv):
                    e = jnp.exp(-jnp.abs(dall[(li + 1) * CHUNK:(li + 2) * CHUNK, ln]))
                    base = C_QM + 3 * (d * n_lv + li)
                    ke = k * e
                    a = a + _dot_nt((q * (e * cst_ref[base])).astype(BF16),
                                    _pair_rhs(ke, cst_ref[base + 1], cst_ref[base + 2])) * cst_ref[C_SAME + li]
                bc = dall[0:CHUNK, ln]
                tot = bc[CHUNK - 1:CHUNK] if d == 0 else bc[0:1]
                qb = (q * jnp.exp(bc)).astype(BF16)
                kb = (k * jnp.exp(dall[(N_WALL - 1) * CHUNK:, ln])).astype(BF16)
                v = gv_ref[sq, :, p * 2 * DV_C:(p + 1) * 2 * DV_C]
                st = sg_ref[sq, d, p]
                outs.append(_dot(a.astype(BF16), _pair_values(v)) + _dot_nt(qb, st.astype(BF16)))
                sg_ref[sq, d, p] = st * jnp.exp(tot) + _dot_tn(v.astype(BF16), kb) * bd
            o_refs[d][0][sq] = jnp.concatenate(outs, axis=1)
            cnt = cst_ref[C_CNT + d]
            outs = []
            for p in range(H_D // 2):
                ln = slice(p * 2 * DK_D, (p + 1) * 2 * DK_D)
                q, k = rq_ref[sq, :, ln], rk_ref[sq, :, ln]
                lg = lg_ref[d, :, ln]
                dec = jnp.exp(lg * cst_ref[C_DIST]) * cst_ref[C_CAUSAL + d]
                a = _dot_nt(q.astype(BF16), _pair_rhs(k, head0, head1)) * dec
                qs = (q * jnp.exp(lg * cnt)).astype(BF16)
                ks = (k * jnp.exp(lg * (CHUNK - cnt))).astype(BF16)
                v = rv_ref[sq, :, p * 2 * DV_D:(p + 1) * 2 * DV_D]
                st = sr_ref[sq, d, p]
                outs.append(_dot(a.astype(BF16), _pair_values(v)) + _dot_nt(qs, st.astype(BF16)))
                sr_ref[sq, d, p] = st * jnp.exp(lg * CHUNK) + _dot_tn(v.astype(BF16), ks) * bd
            o_refs[d][1][sq] = jnp.concatenate(outs, axis=1)

    if emit_state:
        @pl.when(c_id == n_chunks - 1)
        def _():
            sgo_ref[...] = sg_ref[...]
            sro_ref[...] = sr_ref[...]


def _pair_states(s):
    n, _, h, dk, dv = s.shape
    st = jnp.swapaxes(s, -1, -2).reshape(n, 2, h // 2, 2, dv, dk)
    z = jnp.zeros_like(st[:, :, :, 0])
    top = jnp.concatenate([st[:, :, :, 0], z], axis=-1)
    bot = jnp.concatenate([z, st[:, :, :, 1]], axis=-1)
    return jnp.concatenate([top, bot], axis=-2)


def _unpair_states(sp):
    n, _, hp, dv2, dk2 = sp.shape
    dv, dk = dv2 // 2, dk2 // 2
    st = jnp.stack([sp[..., :dv, :dk], sp[..., dv:, dk:]], axis=3).reshape(n, 2, 2 * hp, dv, dk)
    return jnp.swapaxes(st, -1, -2)


def _scan(p, lg_rows, wall, cst, bd, *, n_seq, t_seq, s0=None, emit_state):
    nc = t_seq // CHUNK
    ns = SCAN_SEQS
    p3 = p.reshape(n_seq, t_seq, ODD_OUT)
    fwd = lambda blk: (lambda b, c: (b, c, blk))
    bwd = lambda blk: (lambda b, c: (b, nc - 1 - c, blk))
    in_specs, args = [], []
    for m in (fwd, bwd):
        d = 0 if m is fwd else 1
        in_specs += [pl.BlockSpec((ns, CHUNK, 256), m(0)), pl.BlockSpec((ns, CHUNK, 256), m(1)),
                     pl.BlockSpec((ns, CHUNK, 512), m(1)), pl.BlockSpec((ns, CHUNK, 256), m(ODD_MAIN // 256 + d)),
                     pl.BlockSpec((ns, CHUNK, 256), m(6)), pl.BlockSpec((ns, CHUNK, 256), m(7)),
                     pl.BlockSpec((ns, CHUNK, 512), m(4))]
        args += [p3] * 7
    in_specs += [_resident(wall.shape, lambda b, c: (0, 0, 0)), _resident(cst.shape, lambda b, c: (0, 0, 0)),
                 _resident(bd.shape, lambda b, c: (0, 0)), _resident(lg_rows.shape, lambda b, c: (0, 0, 0))]
    args += [wall, cst, bd, lg_rows]
    st_shape = (ns, 2, H_C // 2, 2 * DV_C, 2 * DK_C)
    st_spec = pl.BlockSpec(st_shape, lambda b, c: (b, 0, 0, 0, 0))
    if s0 is not None:
        in_specs += [st_spec, st_spec]
        args += list(s0)
    out_specs = [pl.BlockSpec((ns, CHUNK, 512), fwd(0)), pl.BlockSpec((ns, CHUNK, 512), fwd(0)),
                 pl.BlockSpec((ns, CHUNK, 512), bwd(0)), pl.BlockSpec((ns, CHUNK, 512), bwd(0))]
    out_shape = [jax.ShapeDtypeStruct((n_seq, t_seq, 512), F32)] * 4
    if emit_state:
        out_specs += [st_spec, st_spec]
        out_shape += [jax.ShapeDtypeStruct((n_seq,) + st_shape[1:], F32)] * 2
    res = pl.pallas_call(
        functools.partial(_scan_kernel, n_chunks=nc, n_seq=ns, has_s0=s0 is not None, emit_state=emit_state),
        grid=(n_seq // ns, nc),
        in_specs=in_specs,
        out_specs=out_specs,
        out_shape=out_shape,
        scratch_shapes=[pltpu.VMEM(st_shape, F32), pltpu.VMEM(st_shape, F32)],
        compiler_params=_cparams("parallel", "arbitrary"),
        name=f"scan_{'s' if s0 is not None else 'p'}",
    )(*args)
    return [r.reshape(n_seq * t_seq, 512) for r in res[:4]] + list(res[4:])


def _odd_post_kernel(ogf_ref, orf_ref, ogb_ref, orb_ref, gr_ref, rr_ref, gn_ref, rn_ref, o_ref):
    og = ogf_ref[...] + ogb_ref[...]
    orr = orf_ref[...] + orb_ref[...]
    gr, rr = gr_ref[...], rr_ref[...]
    outs = []
    for h in range(H_C):
        hc = slice(h * DV_C, (h + 1) * DV_C)
        outs.append(_rms(og[:, hc], gn_ref[...]) * _silu(gr[:, hc]))
    for h in range(H_D):
        hc = slice(h * DV_D, (h + 1) * DV_D)
        xh = orr[:, hc]
        mu = jnp.mean(xh, axis=-1, keepdims=True)
        xc = xh - mu
        var = jnp.mean(xc * xc, axis=-1, keepdims=True)
        outs.append(xc * lax.rsqrt(var + EPS) * rn_ref[...] * _silu(rr[:, hc]))
    o_ref[...] = jnp.concatenate(outs, axis=1).astype(BF16)


def _odd_post(scan_outs, p, gla_norm, ret_norm, i):
    n = p.shape[0]
    row = lambda blk: pl.BlockSpec((TM, 512), lambda b: (b, blk))
    return pl.pallas_call(
        _odd_post_kernel,
        grid=(n // TM,),
        in_specs=[row(0)] * 4 + [row(2), row(5),
                                 pl.BlockSpec((None, 1, DV_C), lambda b: (i, 0, 0)),
                                 pl.BlockSpec((None, 1, DV_D), lambda b: (i, 0, 0))],
        out_specs=pl.BlockSpec((TM, D_MODEL), lambda b: (b, 0)),
        out_shape=jax.ShapeDtypeStruct((n, D_MODEL), BF16),
        compiler_params=_cparams("parallel"),
        name="odd_post",
    )(*scan_outs, p, p, gla_norm, ret_norm)


def _pad_cols(w, width):
    return jnp.pad(w, [(0, 0)] * (w.ndim - 1) + [(0, width - w.shape[-1])])


def _even_weights(even_w_in, mla_w_uq, mla_w_ukv):
    cq, ckv, kr, qd, kd, vd = jnp.split(even_w_in, np.cumsum([Q_LORA, KV_LORA, ROPE_A, 512, 512]).tolist(), axis=-1)
    zeros = lambda n: jnp.zeros(even_w_in.shape[:-1] + (n,), even_w_in.dtype)
    w_in = jnp.concatenate([cq, ckv, zeros(NOPE_A), kr, zeros(HEAD_W - NOPE_A - ROPE_A), qd, kd, vd], axis=-1)
    uq = mla_w_uq.reshape(N_EVEN, Q_LORA, H_A, NOPE_A + ROPE_A)
    w_uq = _pad_cols(uq, HEAD_W).reshape(N_EVEN, Q_LORA, H_A * HEAD_W)
    ukv = mla_w_ukv.reshape(N_EVEN, KV_LORA, H_A, NOPE_A + V_A)
    w_k = _pad_cols(ukv[..., :NOPE_A], HEAD_W).reshape(N_EVEN, KV_LORA, H_A * HEAD_W)
    w_v = ukv[..., NOPE_A:].reshape(N_EVEN, KV_LORA, H_A * V_A)
    w_va = _pad_cols(ukv[..., NOPE_A:], HEAD_W).reshape(N_EVEN, KV_LORA, H_A * HEAD_W)
    return tuple(w.astype(BF16) for w in (w_in, w_uq, w_k, w_v, w_va))


def _odd_weights(odd_w_in, gla_w_gate, gla_b_gate):
    cuts = np.cumsum([256, 256, 512, 2 * GATE_RANK, 512, 256, 256, 512]).tolist()
    gq, gk, gv, glr, gr, rq, rk, rv, rr = jnp.split(odd_w_in, cuts, axis=-1)
    w_in = jnp.concatenate([gq, gk, gv, gr, rq, rk, rv, rr, _pad_cols(glr, LANES)], axis=-1)
    w_gate = jnp.zeros((N_ODD, LANES, 2 * H_C * DK_C), F32)
    for d in range(2):
        w_gate = w_gate.at[:, d * GATE_RANK:(d + 1) * GATE_RANK, d * 256:(d + 1) * 256].set(gla_w_gate[:, d])
    b_gate = gla_b_gate.reshape(N_ODD, 1, 2 * H_C * DK_C)
    return w_in.astype(BF16), w_gate.astype(BF16), b_gate


def kernel(x_prompt, x_sample, cache_mla_ckv, cache_mla_krope, cache_diff_k, cache_diff_v, state_gla, state_ret, c, c_ctx, mod_w, mod_b, norm_g, ffn_w_gate, ffn_w_up, ffn_w_down, even_w_in, mla_q_norm, mla_w_uq, mla_kv_norm, mla_w_ukv, diff_lambda, diff_subln, even_w_out, odd_w_in, gla_w_gate, gla_b_gate, gla_norm, ret_decay, ret_norm, odd_w_out, final_g):
    x = (x_prompt.reshape(P_TOK, D_MODEL), x_sample.reshape(S_TOK, D_MODEL))
    cond8 =jnp.concatenate([c_ctx[None], c, jnp.zeros((N_GROUPS - 1 - DEC_BATCH, D_MODEL), F32)], axis=0)
    mod5 = _adaln(cond8, mod_w, mod_b).reshape(DEPTH, N_MOD, N_GROUPS, 1, D_MODEL)
    norm_g3 = norm_g.reshape(DEPTH * 3, 1, D_MODEL)
    wg, wu, wd = ffn_w_gate.astype(BF16), ffn_w_up.astype(BF16), ffn_w_down.astype(BF16)
    e_w_in, e_w_uq, e_w_k, e_w_v, e_w_va = _even_weights(even_w_in, mla_w_uq, mla_w_ukv)
    o_w_in, o_w_gate, o_b_gate = _odd_weights(odd_w_in, gla_w_gate, gla_b_gate)
    e_w_out, o_w_out = even_w_out.astype(BF16), odd_w_out.astype(BF16)
    q_norm = mla_q_norm.reshape(N_EVEN, 1, Q_LORA)
    kv_norm = mla_kv_norm.reshape(N_EVEN, 1, KV_LORA)
    subln = diff_subln.reshape(N_EVEN, 1, DV_B)
    gla_n = gla_norm.reshape(N_ODD, 1, DV_C)
    ret_n = ret_norm.reshape(N_ODD, 1, DV_D)
    log_gamma = jnp.log1p(-jnp.exp2(-ret_decay))
    wall = jnp.asarray(np.tile(_scan_weights(), (1, 1, 2)), BF16)
    cst = jnp.asarray(_scan_consts())
    bd = jnp.asarray(np.kron(np.eye(2, dtype=np.float32), np.ones((DV_C, DK_C), np.float32)))

    tpos = np.arange(DEC_SEQ, dtype=np.float32)
    pos = {"row": np.floor(tpos / GRID_W), "col": tpos % GRID_W, "t": tpos}
    tab_q = _rope_tables([(NOPE_A, ROPE_A // 2, "row"), (NOPE_A + ROPE_A // 2, ROPE_A // 2, "col")], pos)
    tab_d = _rope_tables([(o, DH_B // 2, key) for o, key in ((0, "row"), (32, "col"), (64, "row"), (96, "col"))], pos)
    tab_r = _rope_tables([(0, DK_D, "t"), (DK_D, DK_D, "t")], pos)

    mix = None
    w_out, w_out_idx = None, None
    new_ckv, new_krope, new_dk, new_dv, new_sg, new_sr = [], [], [], [], [], []
    for l in range(DEPTH):
        i = l // 2
        x = _ffn(x, mod5, norm_g3, wg, wu, wd, l, 0)
        if l % 2 == 0:
            lam_init = 0.8 - 0.6 * math.exp(-0.3 * l)
            qp, kp, vap, qdp, kdp, vdp, ckv, krp, kd32, vd32 = _even_proj(
                x, mod5, norm_g3, e_w_in, q_norm, e_w_uq, kv_norm, e_w_k, e_w_v, l, latent=False)
            new_ckv.append(ckv.reshape(BATCH, SEQ, KV_LORA))
            new_krope.append(krp[:, NOPE_A:NOPE_A + ROPE_A].reshape(BATCH, SEQ, ROPE_A))
            new_dk.append(kd32.reshape(BATCH, SEQ, H_B, 2 * DH_B))
            new_dv.append(vd32.reshape(BATCH, SEQ, H_B, DV_B))
            mix_p = _attention(qp, qdp, [(kp, vap, kdp, vdp)],
                               diff_lambda, subln, i, lam_init, n_batch=BATCH, t_q=SEQ, name=f"attn_l{l}_p")
            qs, ks, vas, qds, kds, vds = _even_proj(
                x, mod5, norm_g3, e_w_in, q_norm, e_w_uq, kv_norm, e_w_k, e_w_va, l, latent=True,
                tables=tab_q + tab_d)
            krp_c = jnp.pad(cache_mla_krope[:, i].reshape(DEC_BATCH * PAST_LEN, ROPE_A),
                            ((0, 0), (NOPE_A, HEAD_W - NOPE_A - ROPE_A)))
            kc, vac = _cache_kv(cache_mla_ckv[:, i].reshape(DEC_BATCH * PAST_LEN, KV_LORA), krp_c, e_w_k, e_w_va, i)
            n_c = DEC_BATCH * PAST_LEN
            kdc = cache_diff_k[:, i].reshape(n_c, H_B * DV_B).astype(BF16)
            vdc = cache_diff_v[:, i].reshape(n_c, H_B * DV_B).astype(BF16)
            mix_s = _attention(qs, qds, [(kc, vac, kdc, vdc), (ks, vas, kds, vds)], diff_lambda, subln, i, lam_init,
                               n_batch=DEC_BATCH, t_q=DEC_SEQ, name=f"attn_l{l}_s")
            w_out, w_out_idx = e_w_out, i
        else:
            pp = _odd_proj(x, mod5, norm_g3, o_w_in, o_w_gate, o_b_gate, l, latent=False)
            lg_rows = jnp.repeat(log_gamma[i], DK_D, axis=-1).reshape(2, 1, H_D * DK_D)
            *outs_p, sg, sr = _scan(pp, lg_rows, wall, cst, bd, n_seq=BATCH, t_seq=SEQ, emit_state=True)
            new_sg.append(_unpair_states(sg))
            new_sr.append(_unpair_states(sr))
            mix_p = _odd_post(outs_p, pp, gla_n, ret_n, i)
            ps = _odd_proj(x, mod5, norm_g3, o_w_in, o_w_gate, o_b_gate, l, latent=True, tables=tab_r)
            s0 = (_pair_states(state_gla[:, i]), _pair_states(state_ret[:, i]))
            outs_s = _scan(ps, lg_rows, wall, cst, bd, n_seq=DEC_BATCH, t_seq=DEC_SEQ, s0=s0, emit_state=False)
            mix_s = _odd_post(outs_s, ps, gla_n, ret_n, i)
            w_out, w_out_idx = o_w_out, i
        x = _ffn(x, mod5, norm_g3, wg, wu, wd, l, 1, mix=(mix_p, mix_s), w_out=w_out, w_out_idx=w_out_idx,
                 final_g=final_g.reshape(1, D_MODEL) if l == DEPTH - 1 else None)

    y_p, y_s = x
    return (y_p.reshape(BATCH, SEQ, D_MODEL), y_s.reshape(DEC_BATCH, DEC_SEQ, D_MODEL),
            jnp.stack(new_ckv, axis=1), jnp.stack(new_krope, axis=1),
            jnp.stack(new_dk, axis=1), jnp.stack(new_dv, axis=1),
            jnp.stack(new_sg, axis=1), jnp.stack(new_sr, axis=1))
```

```python
import functools
import math

import numpy as np
import jax
import jax.numpy as jnp
from jax import lax
from jax.experimental import pallas as pl
from jax.experimental.pallas import tpu as pltpu

F32 = jnp.float32
BF16 = jnp.bfloat16

D_MODEL = 1024
BATCH = 16
SEQ = 256
DEPTH = 4
DEC_BATCH = 4
DEC_SEQ = 2048
PAST_LEN = 512
GRID_W = 64
N_EVEN = (DEPTH + 1) // 2
N_ODD = DEPTH // 2
N_MOD = 9
D_FF = 2816
H_A, NOPE_A, ROPE_A, V_A = 8, 64, 32, 64
Q_LORA, KV_LORA = 384, 256
H_B, DH_B = 4, 64
DV_B = 2 * DH_B
H_C, DK_C, DV_C = 4, 64, 128
GATE_RANK = 16
GATE_TAU = 16.0
H_D, DK_D, DV_D = 4, 64, 128
CHUNK = 64
ROPE_BASE = 10000.0
EPS = 1e-6

LANES = 128
SUBLANES = 8
VMEM_LIMIT_BYTES = 56 * 1024 * 1024

P_TOK = BATCH * SEQ
S_TOK = DEC_BATCH * DEC_SEQ
N_TOK = P_TOK + S_TOK
N_GROUPS = SUBLANES

TM = 512
TM_FFN = 512
FF_CHUNK = 256
TQ = 512
HEAD_W = LANES
LOG2E = math.log2(math.e)
EVEN_W = Q_LORA + KV_LORA + HEAD_W + 3 * H_B * 2 * DH_B
ODD_MAIN = 3072
ODD_W = ODD_MAIN + LANES
ODD_OUT = ODD_MAIN + 2 * H_C * DK_C
SCAN_SEQS = 4
LEVELS = (32, 16, 8, 4, 2, 1)
N_WALL = len(LEVELS) + 2


def _dot(a, b):
    return jnp.dot(a, b, preferred_element_type=F32)


def _dot_nt(a, b):
    return lax.dot_general(a, b, (((1,), (1,)), ((), ())), preferred_element_type=F32)


def _dot_tn(a, b):
    return lax.dot_general(a, b, (((0,), (0,)), ((), ())), preferred_element_type=F32)


def _silu(x):
    return x * (1.0 / (1.0 + jnp.exp(-x)))


def _rms(x, g):
    return x * lax.rsqrt(jnp.mean(x * x, axis=-1, keepdims=True) + EPS) * g


def _modulate(x, g, shift, scale):
    return _rms(x, g) * (1.0 + scale) + shift


def _group_of_block(i, tm):
    return jnp.where(i < P_TOK // tm, 0, 1 + (i - P_TOK // tm) // (DEC_SEQ // tm))


def _cparams(*sem):
    return pltpu.CompilerParams(dimension_semantics=sem, vmem_limit_bytes=VMEM_LIMIT_BYTES)


def _resident(shape, index_map):
    return pl.BlockSpec(shape, index_map, pipeline_mode=pl.Buffered(1))


def _adaln_kernel(c_ref, w_ref, b_ref, o_ref):
    s = _silu(c_ref[...]).astype(BF16)
    o_ref[...] = _dot(s, w_ref[...].astype(BF16)) + b_ref[...]


def _adaln(cond8, mod_w, mod_b):
    mod_b4 = mod_b.reshape(DEPTH, N_MOD, 1, D_MODEL)
    return pl.pallas_call(
        _adaln_kernel,
        grid=(DEPTH, N_MOD),
        in_specs=[
            pl.BlockSpec((N_GROUPS, D_MODEL), lambda l, j: (0, 0)),
            pl.BlockSpec((None, D_MODEL, D_MODEL), lambda l, j: (l, 0, j)),
            pl.BlockSpec((None, None, 1, D_MODEL), lambda l, j: (l, j, 0, 0)),
        ],
        out_specs=pl.BlockSpec((None, None, N_GROUPS, D_MODEL), lambda l, j: (l, j, 0, 0)),
        out_shape=jax.ShapeDtypeStruct((DEPTH, N_MOD, N_GROUPS, D_MODEL), F32),
        compiler_params=_cparams("arbitrary", "arbitrary"),
        name="adaln",
    )(cond8, mod_w, mod_b4)


def _mod_spec(l, j, grp):
    return pl.BlockSpec((None, None, None, 1, D_MODEL), lambda i: (l, j, grp(i), 0, 0))


def _split_specs(width):
    npb = P_TOK // TM_FFN
    return [pl.BlockSpec((TM_FFN, width), lambda i: (jnp.minimum(i, npb - 1), 0)),
            pl.BlockSpec((TM_FFN, width), lambda i: (jnp.maximum(i - npb, 0), 0))]


def _ffn_kernel(*refs, split_x, has_mix, final):
    it = iter(refs)
    is_ctx = pl.program_id(0) < P_TOK // TM_FFN
    pick = lambda p_ref, s_ref: jnp.where(is_ctx, p_ref[...], s_ref[...])
    x = pick(next(it), next(it)) if split_x else next(it)[...]
    if has_mix:
        mix = pick(next(it), next(it))
        wout_ref, gmix_ref = next(it), next(it)
    sh_ref, sc_ref, gt_ref, ng_ref = next(it), next(it), next(it), next(it)
    wg_ref, wu_ref, wd_ref = next(it), next(it), next(it)
    if final:
        fg_ref, yp_ref, ys_ref = next(it), next(it), next(it)
    else:
        o_ref = next(it)

    if has_mix:
        x = x + gmix_ref[...] * _dot(mix, wout_ref[...])
    h = _modulate(x, ng_ref[...], sh_ref[...], sc_ref[...]).astype(BF16)
    acc = jnp.zeros(x.shape, F32)
    for j in range(D_FF // FF_CHUNK):
        cols = slice(j * FF_CHUNK, (j + 1) * FF_CHUNK)
        a = _dot(h, wg_ref[:, cols])
        u = _dot(h, wu_ref[:, cols])
        acc = acc + _dot((_silu(a) * u).astype(BF16), wd_ref[cols, :])
    out = x + 0.5 * gt_ref[...] * acc
    if final:
        y = _rms(out, fg_ref[...])

        @pl.when(is_ctx)
        def _():
            yp_ref[...] = y

        @pl.when(jnp.logical_not(is_ctx))
        def _():
            ys_ref[...] = y
    else:
        o_ref[...] = out


def _ffn(x, mod5, norm_g3, wg, wu, wd, l, k, *, mix=None, w_out=None, w_out_idx=None, final_g=None):
    j = 0 if k == 0 else 6
    grp = functools.partial(_group_of_block, tm=TM_FFN)
    row_spec = pl.BlockSpec((TM_FFN, D_MODEL), lambda i: (i, 0))
    split_x = isinstance(x, tuple)
    in_specs = _split_specs(D_MODEL) if split_x else [row_spec]
    args = list(x) if split_x else [x]
    if mix is not None:
        in_specs += _split_specs(D_MODEL) + [_resident((None, D_MODEL, D_MODEL), lambda i: (w_out_idx, 0, 0)),
                                             _mod_spec(l, 5, grp)]
        args += [*mix, w_out, mod5]
    in_specs += [_mod_spec(l, j, grp), _mod_spec(l, j + 1, grp), _mod_spec(l, j + 2, grp),
                 pl.BlockSpec((None, 1, D_MODEL), lambda i: (3 * l + (0 if k == 0 else 2), 0, 0)),
                 _resident((None, None, D_MODEL, D_FF), lambda i: (l, k, 0, 0)),
                 _resident((None, None, D_MODEL, D_FF), lambda i: (l, k, 0, 0)),
                 _resident((None, None, D_FF, D_MODEL), lambda i: (l, k, 0, 0))]
    args += [mod5, mod5, mod5, norm_g3, wg, wu, wd]
    if final_g is not None:
        in_specs.append(pl.BlockSpec((1, D_MODEL), lambda i: (0, 0)))
        args.append(final_g)
        out_specs = _split_specs(D_MODEL)
        out_shape = [jax.ShapeDtypeStruct((P_TOK, D_MODEL), F32), jax.ShapeDtypeStruct((S_TOK, D_MODEL), F32)]
    else:
        out_specs = row_spec
        out_shape = jax.ShapeDtypeStruct((N_TOK, D_MODEL), F32)
    return pl.pallas_call(
        functools.partial(_ffn_kernel, split_x=split_x, has_mix=mix is not None, final=final_g is not None),
        grid=(N_TOK // TM_FFN,),
        in_specs=in_specs,
        out_specs=out_specs,
        out_shape=out_shape,
        compiler_params=_cparams("arbitrary"),
        name=f"ffn_l{l}_{k}",
    )(*args)


def _rope_tables(subvectors, positions):
    t = next(iter(positions.values())).shape[0]
    cos = np.ones((t, LANES), np.float32)
    sin_m = np.zeros((t, LANES), np.float32)
    sin_p = np.zeros((t, LANES), np.float32)
    for first, n, key in subvectors:
        half = n // 2
        inv = np.float32(ROPE_BASE) ** (-np.arange(half, dtype=np.float32) * np.float32(2.0) / np.float32(n))
        ang = positions[key][:, None] * inv[None, :].astype(np.float32)
        c, s = np.cos(ang).astype(np.float32), np.sin(ang).astype(np.float32)
        cos[:, first:first + n] = np.concatenate([c, c], axis=1)
        sin_m[:, first:first + half] = -s
        sin_p[:, first + half:first + n] = s
    return jnp.asarray(cos), jnp.asarray(sin_m), jnp.asarray(sin_p)


def _tile_lanes(t, width):
    return t if width == LANES else jnp.concatenate([t] * (width // LANES), axis=1)


def _rope(y, cos, sin_m, sin_p, half):
    w = y.shape[1]
    c, sm, sp = (_tile_lanes(t, w) for t in (cos, sin_m, sin_p))
    return y * c + pltpu.roll(y, w - half, 1) * sm + pltpu.roll(y, half, 1) * sp


def _ones_lane(lane, width):
    idx = lax.broadcasted_iota(jnp.int32, (1, width), 1)
    return jnp.where((idx & (LANES - 1)) == lane, 1.0, 0.0)


def _even_proj_kernel(*refs, rope, cache):
    it = iter(refs)
    x_ref, sh_ref, sc_ref, ng_ref = next(it), next(it), next(it), next(it)
    win_ref, qn_ref, wuq_ref, kvn_ref, wk_ref, wv_ref = (next(it) for _ in range(6))
    if rope:
        tq = [next(it)[...] for _ in range(3)]
        td = [next(it)[...] for _ in range(3)]
    q_ref, k_ref, va_ref, qd_ref, kd_ref, vd_ref = (next(it) for _ in range(6))
    if cache:
        ckv_ref, krp_ref, kd32_ref, vd32_ref = (next(it) for _ in range(4))

    h = _modulate(x_ref[...], ng_ref[...], sh_ref[...], sc_ref[...]).astype(BF16)
    p = _dot(h, win_ref[...])
    o_ckv = Q_LORA
    o_krp = o_ckv + KV_LORA
    o_qd = o_krp + HEAD_W
    o_kd = o_qd + H_B * 2 * DH_B
    o_vd = o_kd + H_B * 2 * DH_B
    cqn = _rms(p[:, :Q_LORA], qn_ref[...]).astype(BF16)
    q = _dot(cqn, wuq_ref[...])
    ckvn = _rms(p[:, o_ckv:o_krp], kvn_ref[...])
    ckvb = ckvn.astype(BF16)
    krp = p[:, o_krp:o_qd]
    qd = p[:, o_qd:o_kd]
    kd = p[:, o_kd:o_vd]
    vd = p[:, o_vd:]
    if cache:
        ckv_ref[...] = ckvn
        krp_ref[...] = krp
        kd32_ref[...] = kd
        vd32_ref[...] = vd
    if rope:
        q = _rope(q, *tq, half=ROPE_A // 4)
        krp = _rope(krp, *tq, half=ROPE_A // 4)
        qd = _rope(qd, *td, half=DH_B // 4)
        kd = _rope(kd, *td, half=DH_B // 4)
    q_ref[...] = (q * (LOG2E * (NOPE_A + ROPE_A) ** -0.5)).astype(BF16)
    k_ref[...] = (_dot(ckvb, wk_ref[...]) + _tile_lanes(krp, H_A * HEAD_W)).astype(BF16)
    va = _dot(ckvb, wv_ref[...])
    if rope:
        va = va + _ones_lane(V_A, H_A * HEAD_W)
    va_ref[...] = va.astype(BF16)
    qd_ref[...] = (qd * (LOG2E * DH_B ** -0.5)).astype(BF16)
    kd_ref[...] = kd.astype(BF16)
    vd_ref[...] = vd.astype(BF16)


def _even_proj(x, mod5, norm_g3, w_in, q_norm, w_uq, kv_norm, w_k, w_v, l, *, latent, tables=None):
    i = l // 2
    n_rows = S_TOK if latent else P_TOK
    first = P_TOK // TM if latent else 0
    grp = (lambda b: 1 + b // (DEC_SEQ // TM)) if latent else (lambda b: 0)
    row_spec = lambda w: pl.BlockSpec((TM, w), lambda b: (b, 0))
    const = lambda shape: _resident((None,) + shape, lambda b: (i,) + (0,) * len(shape))
    in_specs = [pl.BlockSpec((TM, D_MODEL), lambda b: (first + b, 0)),
                _mod_spec(l, 3, grp), _mod_spec(l, 4, grp),
                pl.BlockSpec((None, 1, D_MODEL), lambda b: (3 * l + 1, 0, 0)),
                const((D_MODEL, EVEN_W)), const((1, Q_LORA)), const((Q_LORA, H_A * HEAD_W)),
                const((1, KV_LORA)), const((KV_LORA, H_A * HEAD_W)), const((KV_LORA, w_v.shape[-1]))]
    args = [x, mod5, mod5, norm_g3, w_in, q_norm, w_uq, kv_norm, w_k, w_v]
    if latent:
        in_specs += [pl.BlockSpec((TM, LANES), lambda b: (b % (DEC_SEQ // TM), 0))] * 6
        args += list(tables)
    widths = [H_A * HEAD_W, H_A * HEAD_W, w_v.shape[-1], H_B * DV_B, H_B * DV_B, H_B * DV_B]
    out_specs = [row_spec(w) for w in widths]
    out_shape = [jax.ShapeDtypeStruct((n_rows, w), BF16) for w in widths]
    if not latent:
        cache_w = [KV_LORA, HEAD_W, H_B * DV_B, H_B * DV_B]
        out_specs += [row_spec(w) for w in cache_w]
        out_shape += [jax.ShapeDtypeStruct((n_rows, w), F32) for w in cache_w]
    return pl.pallas_call(
        functools.partial(_even_proj_kernel, rope=latent, cache=not latent),
        grid=(n_rows // TM,),
        in_specs=in_specs,
        out_specs=out_specs,
        out_shape=out_shape,
        compiler_params=_cparams("parallel"),
        name=f"even_proj_l{l}_{'s' if latent else 'p'}",
    )(*args)


def _cache_kv_kernel(ckv_ref, krp_ref, wk_ref, wv_ref, k_ref, va_ref):
    c = ckv_ref[...].astype(BF16)
    k_ref[...] = (_dot(c, wk_ref[...]) + _tile_lanes(krp_ref[...], H_A * HEAD_W)).astype(BF16)
    va_ref[...] = (_dot(c, wv_ref[...]) + _ones_lane(V_A, H_A * HEAD_W)).astype(BF16)


def _cache_kv(ckv, krp, w_k, w_v, i):
    n = DEC_BATCH * PAST_LEN
    return pl.pallas_call(
        _cache_kv_kernel,
        grid=(n // PAST_LEN,),
        in_specs=[pl.BlockSpec((PAST_LEN, KV_LORA), lambda b: (b, 0)),
                  pl.BlockSpec((PAST_LEN, HEAD_W), lambda b: (b, 0)),
                  _resident((None, KV_LORA, H_A * HEAD_W), lambda b: (i, 0, 0)),
                  _resident((None, KV_LORA, H_A * HEAD_W), lambda b: (i, 0, 0))],
        out_specs=[pl.BlockSpec((PAST_LEN, H_A * HEAD_W), lambda b: (b, 0)),
                   pl.BlockSpec((PAST_LEN, H_A * HEAD_W), lambda b: (b, 0))],
        out_shape=[jax.ShapeDtypeStruct((n, H_A * HEAD_W), BF16),
                   jax.ShapeDtypeStruct((n, H_A * HEAD_W), BF16)],
        compiler_params=_cparams("parallel"),
        name=f"cache_kv_{i}",
    )(ckv, krp, w_k, w_v)


def _exp2_parts(s):
    m = functools.reduce(jnp.maximum, [jnp.max(si, axis=-1, keepdims=True) for si in s])
    return [jnp.exp2(si - m) for si in s]


def _rowsum(ps):
    return functools.reduce(jnp.add, [jnp.sum(p, axis=-1, keepdims=True) for p in ps])


def _pv(ps, vs):
    return functools.reduce(jnp.add, [_dot(p.astype(BF16), v) for p, v in zip(ps, vs)])


def _attn_kernel(*refs, lam_init, n_seg, ones_col):
    q_ref, qd_ref = refs[:2]
    segs = [refs[2 + 4 * g:6 + 4 * g] for g in range(n_seg)]
    lam_ref, sub_ref, o_ref = refs[2 + 4 * n_seg:]
    lp = lam_ref[...]
    lam = (jnp.exp(jnp.sum(lp[0:1] * lp[1:2], axis=-1, keepdims=True))
           - jnp.exp(jnp.sum(lp[2:3] * lp[3:4], axis=-1, keepdims=True)) + lam_init)
    first = lax.broadcasted_iota(jnp.int32, (1, DV_B), 1) < DH_B
    va_w = HEAD_W if ones_col else V_A

    def scores(task):
        kind, h = task
        if kind == "mla":
            cols = slice(h * HEAD_W, (h + 1) * HEAD_W)
            return ([_dot_nt(q_ref[:, cols], sg[0][:, cols]) for sg in segs],)
        cols = slice(h * DV_B, (h + 1) * DV_B)
        qh = qd_ref[:, cols].astype(F32)
        q1 = jnp.where(first, qh, 0.0).astype(BF16)
        q2 = jnp.where(first, 0.0, qh).astype(BF16)
        return tuple([_dot_nt(qx, sg[2][:, cols]) for sg in segs] for qx in (q1, q2))

    tasks = [("mla", h) for h in range(H_A)] + [("diff", h) for h in range(H_B)]
    outs = []
    ahead = scores(tasks[0])
    for t, (kind, h) in enumerate(tasks):
        cur = ahead
        if t + 1 < len(tasks):
            ahead = scores(tasks[t + 1])
        if kind == "mla":
            p = _exp2_parts(cur[0])
            acc = _pv(p, [sg[1][:, h * va_w:(h + 1) * va_w] for sg in segs])
            inv = 1.0 / (acc[:, V_A:V_A + 1] if ones_col else _rowsum(p))
            outs.append(acc[:, :V_A] * inv)
        else:
            p1, p2 = _exp2_parts(cur[0]), _exp2_parts(cur[1])
            w1, w2 = 1.0 / _rowsum(p1), lam / _rowsum(p2)
            o = _pv([a * w1 - b * w2 for a, b in zip(p1, p2)],
                    [sg[3][:, h * DV_B:(h + 1) * DV_B] for sg in segs])
            outs.append(_rms(o, sub_ref[...]) * (1.0 - lam_init))
    o_ref[...] = jnp.concatenate(outs, axis=1).astype(BF16)


def _attention(q, qd, segs, lam_p, subln, i, lam_init, *, n_batch, t_q, name):
    tq = min(TQ, t_q)
    nq = t_q // tq
    q_spec = lambda w: pl.BlockSpec((tq, w), lambda b, j: (b * nq + j, 0))
    in_specs = [q_spec(H_A * HEAD_W), q_spec(H_B * DV_B)]
    args = [q, qd]
    for seg in segs:
        t_seg = seg[0].shape[0] // n_batch
        for a in seg:
            in_specs.append(pl.BlockSpec((t_seg, a.shape[1]), lambda b, j: (b, 0), pipeline_mode=pl.Buffered(1)))
            args.append(a)
    in_specs += [pl.BlockSpec((None, 4, DH_B), lambda b, j: (i, 0, 0)),
                 pl.BlockSpec((None, 1, DV_B), lambda b, j: (i, 0, 0))]
    args += [lam_p, subln]
    ones_col = segs[0][1].shape[1] == H_A * HEAD_W
    return pl.pallas_call(
        functools.partial(_attn_kernel, lam_init=lam_init, n_seg=len(segs), ones_col=ones_col),
        grid=(n_batch, nq),
        in_specs=in_specs,
        out_specs=q_spec(D_MODEL),
        out_shape=jax.ShapeDtypeStruct((n_batch * t_q, D_MODEL), BF16),
        compiler_params=_cparams("parallel", "arbitrary"),
        name=name,
    )(*args)


def _odd_proj_kernel(*refs, rope):
    it = iter(refs)
    x_ref, sh_ref, sc_ref, ng_ref, win_ref, wgate_ref, bgate_ref = (next(it) for _ in range(7))
    if rope:
        tr = [next(it)[...] for _ in range(3)]
    o_ref = next(it)

    h = _modulate(x_ref[...], ng_ref[...], sh_ref[...], sc_ref[...]).astype(BF16)
    p = _dot(h, win_ref[...])
    o_ref[:, 0:256] = p[:, 0:256] * (DK_C ** -0.5)
    o_ref[:, 256:1536] = p[:, 256:1536]
    rq = p[:, 1536:1792]
    rk = p[:, 1792:2048] * (DK_D ** -0.5)
    if rope:
        rq = _rope(rq, *tr, half=DK_D // 2)
        rk = _rope(rk, *tr, half=DK_D // 2)
    o_ref[:, 1536:1792] = rq
    o_ref[:, 1792:2048] = rk
    o_ref[:, 2048:ODD_MAIN] = p[:, 2048:ODD_MAIN]
    logit = _dot(p[:, ODD_MAIN:].astype(BF16), wgate_ref[...]) + bgate_ref[...]
    o_ref[:, ODD_MAIN:] = (jnp.minimum(logit, 0.0) - jnp.log1p(jnp.exp(-jnp.abs(logit)))) * (1.0 / GATE_TAU)


def _odd_proj(x, mod5, norm_g3, w_in, w_gate, b_gate, l, *, latent, tables=None):
    i = l // 2
    n_rows = S_TOK if latent else P_TOK
    first = P_TOK // TM if latent else 0
    grp = (lambda b: 1 + b // (DEC_SEQ // TM)) if latent else (lambda b: 0)
    in_specs = [pl.BlockSpec((TM, D_MODEL), lambda b: (first + b, 0)),
                _mod_spec(l, 3, grp), _mod_spec(l, 4, grp),
                pl.BlockSpec((None, 1, D_MODEL), lambda b: (3 * l + 1, 0, 0)),
                _resident((None, D_MODEL, ODD_W), lambda b: (i, 0, 0)),
                _resident((None, LANES, 2 * H_C * DK_C), lambda b: (i, 0, 0)),
                _resident((None, 1, 2 * H_C * DK_C), lambda b: (i, 0, 0))]
    args = [x, mod5, mod5, norm_g3, w_in, w_gate, b_gate]
    if latent:
        in_specs += [pl.BlockSpec((TM, LANES), lambda b: (b % (DEC_SEQ // TM), 0))] * 3
        args += list(tables)
    return pl.pallas_call(
        functools.partial(_odd_proj_kernel, rope=latent),
        grid=(n_rows // TM,),
        in_specs=in_specs,
        out_specs=pl.BlockSpec((TM, ODD_OUT), lambda b: (b, 0)),
        out_shape=jax.ShapeDtypeStruct((n_rows, ODD_OUT), F32),
        compiler_params=_cparams("parallel"),
        name=f"odd_proj_l{l}_{'s' if latent else 'p'}",
    )(*args)


def _scan_weights():
    c = CHUNK
    t = np.arange(c)
    out = np.zeros((2, N_WALL * c, c), np.float32)
    for d in range(2):
        run = (t[None, :] <= t[:, None]) if d == 0 else (t[None, :] >= t[:, None])
        run = run.astype(np.float32)
        out[d, 0:c] = run
        for li, b in enumerate(LEVELS):
            base = (t // (2 * b)) * (2 * b)
            r = base + (b - 1 if d == 0 else b)
            out[d, (li + 1) * c:(li + 2) * c] = run - run[r]
        out[d, (N_WALL - 1) * c:] = 1.0 - run
    return out


C_SAME = 0
C_QM = C_SAME + len(LEVELS) + 1
C_DIST = C_QM + 3 * 2 * len(LEVELS)
C_CAUSAL = C_DIST + 1
C_CNT = C_CAUSAL + 2
N_CONST = C_CNT + 2


def _scan_consts():
    c = CHUNK
    t = np.arange(c)[:, None]
    s = np.arange(2 * c)[None, :] % c
    head0 = (np.arange(2 * c)[None, :] < c)
    out = np.zeros((N_CONST, c, 2 * c), np.float32)
    for li, b in enumerate(LEVELS):
        out[C_SAME + li] = (t // (2 * b)) == (s // (2 * b))
        later = (t % (2 * b)) >= b
        for d in range(2):
            is_q = later if d == 0 else ~later
            base = C_QM + 3 * (d * len(LEVELS) + li)
            out[base] = is_q
            out[base + 1] = ~is_q & head0
            out[base + 2] = ~is_q & ~head0
    out[C_SAME + len(LEVELS)] = (t == s)
    out[C_DIST] = np.abs(t - s)
    out[C_CAUSAL] = s <= t
    out[C_CAUSAL + 1] = s >= t
    out[C_CNT] = t + 1
    out[C_CNT + 1] = c - t
    return out


def _pair_rhs(kf, m0, m1):
    return jnp.concatenate([kf * m0, kf * m1], axis=0).astype(BF16)


def _pair_values(v):
    left = lax.broadcasted_iota(jnp.int32, (1, v.shape[1]), 1) < v.shape[1] // 2
    return jnp.concatenate([jnp.where(left, v, 0.0), jnp.where(left, 0.0, v)], axis=0).astype(BF16)


def _scan_kernel(*refs, n_chunks, n_seq, has_s0, emit_state):
    it = iter(refs)
    ins = [[next(it) for _ in range(7)] for _ in range(2)]
    wall_ref, cst_ref, bd_ref, lg_ref = next(it), next(it), next(it), next(it)
    if has_s0:
        s0g_ref, s0r_ref = next(it), next(it)
    o_refs = [[next(it), next(it)] for _ in range(2)]
    if emit_state:
        sgo_ref, sro_ref = next(it), next(it)
    sg_ref, sr_ref = next(it), next(it)
    c_id = pl.program_id(1)

    @pl.when(c_id == 0)
    def _():
        if has_s0:
            sg_ref[...] = s0g_ref[...]
            sr_ref[...] = s0r_ref[...]
        else:
            sg_ref[...] = jnp.zeros(sg_ref.shape, F32)
            sr_ref[...] = jnp.zeros(sr_ref.shape, F32)

    n_lv = len(LEVELS)
    lane = lax.broadcasted_iota(jnp.int32, (1, 2 * DK_C), 1)
    head0 = jnp.where(lane < DK_C, 1.0, 0.0)
    head1 = 1.0 - head0
    bd = bd_ref[...]
    streams = [(sq, d) for sq in range(n_seq) for d in range(2)]
    dall = {}
    for sq, d in streams:
        g = ins[d][3][sq]
        g_hi = g.astype(BF16)
        g_lo = (g - g_hi.astype(F32)).astype(BF16)
        dall[sq, d] = _dot(wall_ref[d], jnp.concatenate([g_hi, g_lo], axis=0))
    outs = {s: [] for s in streams}
    for p in range(H_C // 2):
        ln = slice(p * 2 * DK_C, (p + 1) * 2 * DK_C)
        qk = {(sq, d): (ins[d][0][sq, :, ln], ins[d][1][sq, :, ln]) for sq, d in streams}
        a = {s: _dot_nt(qk[s][0].astype(BF16), _pair_rhs(qk[s][1], head0, head1)) * cst_ref[C_SAME + n_lv]
             for s in streams}
        for li in range(n_lv):
            for s in streams:
                q, k = qk[s]
                e = jnp.exp(-jnp.abs(dall[s][(li + 1) * CHUNK:(li + 2) * CHUNK, ln]))
                base = C_QM + 3 * (s[1] * n_lv + li)
                a[s] = a[s] + _dot_nt((q * (e * cst_ref[base])).astype(BF16),
                                      _pair_rhs(k * e, cst_ref[base + 1], cst_ref[base + 2])) * cst_ref[C_SAME + li]
        for s in streams:
            sq, d = s
            q, k = qk[s]
            bc = dall[s][0:CHUNK, ln]
            tot = bc[CHUNK - 1:CHUNK] if d == 0 else bc[0:1]
            qb = (q * jnp.exp(bc)).astype(BF16)
            kb = (k * jnp.exp(dall[s][(N_WALL - 1) * CHUNK:, ln])).astype(BF16)
            v = ins[d][2][sq, :, p * 2 * DV_C:(p + 1) * 2 * DV_C]
            st = sg_ref[sq, d, p]
            outs[s].append(_dot(a[s].astype(BF16), _pair_values(v)) + _dot_nt(qb, st.astype(BF16)))
            sg_ref[sq, d, p] = st * jnp.exp(tot) + _dot_tn(v.astype(BF16), kb) * bd
    for sq, d in streams:
        o_refs[d][0][sq] = jnp.concatenate(outs[sq, d], axis=1)
    outs = {s: [] for s in streams}
    for p in range(H_D // 2):
        ln = slice(p * 2 * DK_D, (p + 1) * 2 * DK_D)
        for d in range(2):
            lg = lg_ref[d, :, ln]
            cnt = cst_ref[C_CNT + d]
            dec = jnp.exp(lg * cst_ref[C_DIST]) * cst_ref[C_CAUSAL + d]
            q_dec, k_dec, s_dec = jnp.exp(lg * cnt), jnp.exp(lg * (CHUNK - cnt)), jnp.exp(lg * CHUNK)
            for sq in range(n_seq):
                q, k = ins[d][4][sq, :, ln], ins[d][5][sq, :, ln]
                a = _dot_nt(q.astype(BF16), _pair_rhs(k, head0, head1)) * dec
                v = ins[d][6][sq, :, p * 2 * DV_D:(p + 1) * 2 * DV_D]
                st = sr_ref[sq, d, p]
                outs[sq, d].append(_dot(a.astype(BF16), _pair_values(v)) + _dot_nt((q * q_dec).astype(BF16), st.astype(BF16)))
                sr_ref[sq, d, p] = st * s_dec + _dot_tn(v.astype(BF16), (k * k_dec).astype(BF16)) * bd
    for sq, d in streams:
        o_refs[d][1][sq] = jnp.concatenate(outs[sq, d], axis=1)

    if emit_state:
        @pl.when(c_id == n_chunks - 1)
        def _():
            sgo_ref[...] = sg_ref[...]
            sro_ref[...] = sr_ref[...]


def _pair_states(s):
    n, _, h, dk, dv = s.shape
    st = jnp.swapaxes(s, -1, -2).reshape(n, 2, h // 2, 2, dv, dk)
    z = jnp.zeros_like(st[:, :, :, 0])
    top = jnp.concatenate([st[:, :, :, 0], z], axis=-1)
    bot = jnp.concatenate([z, st[:, :, :, 1]], axis=-1)
    return jnp.concatenate([top, bot], axis=-2)


def _unpair_states(sp):
    n, _, hp, dv2, dk2 = sp.shape
    dv, dk = dv2 // 2, dk2 // 2
    st = jnp.stack([sp[..., :dv, :dk], sp[..., dv:, dk:]], axis=3).reshape(n, 2, 2 * hp, dv, dk)
    return jnp.swapaxes(st, -1, -2)


def _scan(p, lg_rows, wall, cst, bd, *, n_seq, t_seq, s0=None, emit_state):
    nc = t_seq // CHUNK
    ns = SCAN_SEQS
    p3 = p.reshape(n_seq, t_seq, ODD_OUT)
    fwd = lambda blk: (lambda b, c: (b, c, blk))
    bwd = lambda blk: (lambda b, c: (b, nc - 1 - c, blk))
    in_specs, args = [], []
    for m in (fwd, bwd):
        d = 0 if m is fwd else 1
        in_specs += [pl.BlockSpec((ns, CHUNK, 256), m(0)), pl.BlockSpec((ns, CHUNK, 256), m(1)),
                     pl.BlockSpec((ns, CHUNK, 512), m(1)), pl.BlockSpec((ns, CHUNK, 256), m(ODD_MAIN // 256 + d)),
                     pl.BlockSpec((ns, CHUNK, 256), m(6)), pl.BlockSpec((ns, CHUNK, 256), m(7)),
                     pl.BlockSpec((ns, CHUNK, 512), m(4))]
        args += [p3] * 7
    in_specs += [_resident(wall.shape, lambda b, c: (0, 0, 0)), _resident(cst.shape, lambda b, c: (0, 0, 0)),
                 _resident(bd.shape, lambda b, c: (0, 0)), _resident(lg_rows.shape, lambda b, c: (0, 0, 0))]
    args += [wall, cst, bd, lg_rows]
    st_shape = (ns, 2, H_C // 2, 2 * DV_C, 2 * DK_C)
    st_spec = pl.BlockSpec(st_shape, lambda b, c: (b, 0, 0, 0, 0))
    if s0 is not None:
        in_specs += [st_spec, st_spec]
        args += list(s0)
    out_specs = [pl.BlockSpec((ns, CHUNK, 512), fwd(0)), pl.BlockSpec((ns, CHUNK, 512), fwd(0)),
                 pl.BlockSpec((ns, CHUNK, 512), bwd(0)), pl.BlockSpec((ns, CHUNK, 512), bwd(0))]
    out_shape = [jax.ShapeDtypeStruct((n_seq, t_seq, 512), F32)] * 4
    if emit_state:
        out_specs += [st_spec, st_spec]
        out_shape += [jax.ShapeDtypeStruct((n_seq,) + st_shape[1:], F32)] * 2
    res = pl.pallas_call(
        functools.partial(_scan_kernel, n_chunks=nc, n_seq=ns, has_s0=s0 is not None, emit_state=emit_state),
        grid=(n_seq // ns, nc),
        in_specs=in_specs,
        out_specs=out_specs,
        out_shape=out_shape,
        scratch_shapes=[pltpu.VMEM(st_shape, F32), pltpu.VMEM(st_shape, F32)],
        compiler_params=_cparams("parallel", "arbitrary"),
        name=f"scan_{'s' if s0 is not None else 'p'}",
    )(*args)
    return [r.reshape(n_seq * t_seq, 512) for r in res[:4]] + list(res[4:])


def _odd_post_kernel(ogf_ref, orf_ref, ogb_ref, orb_ref, gr_ref, rr_ref, gn_ref, rn_ref, o_ref):
    og = ogf_ref[...] + ogb_ref[...]
    orr = orf_ref[...] + orb_ref[...]
    gr, rr = gr_ref[...], rr_ref[...]
    outs = []
    for h in range(H_C):
        hc = slice(h * DV_C, (h + 1) * DV_C)
        outs.append(_rms(og[:, hc], gn_ref[...]) * _silu(gr[:, hc]))
    for h in range(H_D):
        hc = slice(h * DV_D, (h + 1) * DV_D)
        xh = orr[:, hc]
        mu = jnp.mean(xh, axis=-1, keepdims=True)
        xc = xh - mu
        var = jnp.mean(xc * xc, axis=-1, keepdims=True)
        outs.append(xc * lax.rsqrt(var + EPS) * rn_ref[...] * _silu(rr[:, hc]))
    o_ref[...] = jnp.concatenate(outs, axis=1).astype(BF16)


def _odd_post(scan_outs, p, gla_norm, ret_norm, i):
    n = p.shape[0]
    row = lambda blk: pl.BlockSpec((TM, 512), lambda b: (b, blk))
    return pl.pallas_call(
        _odd_post_kernel,
        grid=(n // TM,),
        in_specs=[row(0)] * 4 + [row(2), row(5),
                                 pl.BlockSpec((None, 1, DV_C), lambda b: (i, 0, 0)),
                                 pl.BlockSpec((None, 1, DV_D), lambda b: (i, 0, 0))],
        out_specs=pl.BlockSpec((TM, D_MODEL), lambda b: (b, 0)),
        out_shape=jax.ShapeDtypeStruct((n, D_MODEL), BF16),
        compiler_params=_cparams("parallel"),
        name="odd_post",
    )(*scan_outs, p, p, gla_norm, ret_norm)


def _pad_cols(w, width):
    return jnp.pad(w, [(0, 0)] * (w.ndim - 1) + [(0, width - w.shape[-1])])


def _even_weights(even_w_in, mla_w_uq, mla_w_ukv):
    cq, ckv, kr, qd, kd, vd = jnp.split(even_w_in, np.cumsum([Q_LORA, KV_LORA, ROPE_A, 512, 512]).tolist(), axis=-1)
    zeros = lambda n: jnp.zeros(even_w_in.shape[:-1] + (n,), even_w_in.dtype)
    w_in = jnp.concatenate([cq, ckv, zeros(NOPE_A), kr, zeros(HEAD_W - NOPE_A - ROPE_A), qd, kd, vd], axis=-1)
    uq = mla_w_uq.reshape(N_EVEN, Q_LORA, H_A, NOPE_A + ROPE_A)
    w_uq = _pad_cols(uq, HEAD_W).reshape(N_EVEN, Q_LORA, H_A * HEAD_W)
    ukv = mla_w_ukv.reshape(N_EVEN, KV_LORA, H_A, NOPE_A + V_A)
    w_k = _pad_cols(ukv[..., :NOPE_A], HEAD_W).reshape(N_EVEN, KV_LORA, H_A * HEAD_W)
    w_v = ukv[..., NOPE_A:].reshape(N_EVEN, KV_LORA, H_A * V_A)
    w_va = _pad_cols(ukv[..., NOPE_A:], HEAD_W).reshape(N_EVEN, KV_LORA, H_A * HEAD_W)
    return tuple(w.astype(BF16) for w in (w_in, w_uq, w_k, w_v, w_va))


def _odd_weights(odd_w_in, gla_w_gate, gla_b_gate):
    cuts = np.cumsum([256, 256, 512, 2 * GATE_RANK, 512, 256, 256, 512]).tolist()
    gq, gk, gv, glr, gr, rq, rk, rv, rr = jnp.split(odd_w_in, cuts, axis=-1)
    w_in = jnp.concatenate([gq, gk, gv, gr, rq, rk, rv, rr, _pad_cols(glr, LANES)], axis=-1)
    w_gate = jnp.zeros((N_ODD, LANES, 2 * H_C * DK_C), F32)
    for d in range(2):
        w_gate = w_gate.at[:, d * GATE_RANK:(d + 1) * GATE_RANK, d * 256:(d + 1) * 256].set(gla_w_gate[:, d])
    b_gate = gla_b_gate.reshape(N_ODD, 1, 2 * H_C * DK_C)
    return w_in.astype(BF16), w_gate.astype(BF16), b_gate


def kernel(x_prompt, x_sample, cache_mla_ckv, cache_mla_krope, cache_diff_k, cache_diff_v, state_gla, state_ret, c, c_ctx, mod_w, mod_b, norm_g, ffn_w_gate, ffn_w_up, ffn_w_down, even_w_in, mla_q_norm, mla_w_uq, mla_kv_norm, mla_w_ukv, diff_lambda, diff_subln, even_w_out, odd_w_in, gla_w_gate, gla_b_gate, gla_norm, ret_decay, ret_norm, odd_w_out, final_g):
    x = (x_prompt.reshape(P_TOK, D_MODEL), x_sample.reshape(S_TOK, D_MODEL))
    cond8 =jnp.concatenate([c_ctx[None], c, jnp.zeros((N_GROUPS - 1 - DEC_BATCH, D_MODEL), F32)], axis=0)
    mod5 = _adaln(cond8, mod_w, mod_b).reshape(DEPTH, N_MOD, N_GROUPS, 1, D_MODEL)
    norm_g3 = norm_g.reshape(DEPTH * 3, 1, D_MODEL)
    wg, wu, wd = ffn_w_gate.astype(BF16), ffn_w_up.astype(BF16), ffn_w_down.astype(BF16)
    e_w_in, e_w_uq, e_w_k, e_w_v, e_w_va = _even_weights(even_w_in, mla_w_uq, mla_w_ukv)
    o_w_in, o_w_gate, o_b_gate = _odd_weights(odd_w_in, gla_w_gate, gla_b_gate)
    e_w_out, o_w_out = even_w_out.astype(BF16), odd_w_out.astype(BF16)
    q_norm = mla_q_norm.reshape(N_EVEN, 1, Q_LORA)
    kv_norm = mla_kv_norm.reshape(N_EVEN, 1, KV_LORA)
    subln = diff_subln.reshape(N_EVEN, 1, DV_B)
    gla_n = gla_norm.reshape(N_ODD, 1, DV_C)
    ret_n = ret_norm.reshape(N_ODD, 1, DV_D)
    log_gamma = jnp.log1p(-jnp.exp2(-ret_decay))
    wall = jnp.asarray(np.tile(_scan_weights(), (1, 1, 2)), BF16)
    cst = jnp.asarray(_scan_consts())
    bd = jnp.asarray(np.kron(np.eye(2, dtype=np.float32), np.ones((DV_C, DK_C), np.float32)))

    tpos = np.arange(DEC_SEQ, dtype=np.float32)
    pos = {"row": np.floor(tpos / GRID_W), "col": tpos % GRID_W, "t": tpos}
    tab_q = _rope_tables([(NOPE_A, ROPE_A // 2, "row"), (NOPE_A + ROPE_A // 2, ROPE_A // 2, "col")], pos)
    tab_d = _rope_tables([(o, DH_B // 2, key) for o, key in ((0, "row"), (32, "col"), (64, "row"), (96, "col"))], pos)
    tab_r = _rope_tables([(0, DK_D, "t"), (DK_D, DK_D, "t")], pos)

    mix = None
    w_out, w_out_idx = None, None
    new_ckv, new_krope, new_dk, new_dv, new_sg, new_sr = [], [], [], [], [], []
    for l in range(DEPTH):
        i = l // 2
        x = _ffn(x, mod5, norm_g3, wg, wu, wd, l, 0)
        if l % 2 == 0:
            lam_init = 0.8 - 0.6 * math.exp(-0.3 * l)
            qp, kp, vap, qdp, kdp, vdp, ckv, krp, kd32, vd32 = _even_proj(
                x, mod5, norm_g3, e_w_in, q_norm, e_w_uq, kv_norm, e_w_k, e_w_v, l, latent=False)
            new_ckv.append(ckv.reshape(BATCH, SEQ, KV_LORA))
            new_krope.append(krp[:, NOPE_A:NOPE_A + ROPE_A].reshape(BATCH, SEQ, ROPE_A))
            new_dk.append(kd32.reshape(BATCH, SEQ, H_B, 2 * DH_B))
            new_dv.append(vd32.reshape(BATCH, SEQ, H_B, DV_B))
            mix_p = _attention(qp, qdp, [(kp, vap, kdp, vdp)],
                               diff_lambda, subln, i, lam_init, n_batch=BATCH, t_q=SEQ, name=f"attn_l{l}_p")
            qs, ks, vas, qds, kds, vds = _even_proj(
                x, mod5, norm_g3, e_w_in, q_norm, e_w_uq, kv_norm, e_w_k, e_w_va, l, latent=True,
                tables=tab_q + tab_d)
            krp_c = jnp.pad(cache_mla_krope[:, i].reshape(DEC_BATCH * PAST_LEN, ROPE_A),
                            ((0, 0), (NOPE_A, HEAD_W - NOPE_A - ROPE_A)))
            kc, vac = _cache_kv(cache_mla_ckv[:, i].reshape(DEC_BATCH * PAST_LEN, KV_LORA), krp_c, e_w_k, e_w_va, i)
            n_c = DEC_BATCH * PAST_LEN
            kdc = cache_diff_k[:, i].reshape(n_c, H_B * DV_B).astype(BF16)
            vdc = cache_diff_v[:, i].reshape(n_c, H_B * DV_B).astype(BF16)
            mix_s = _attention(qs, qds, [(kc, vac, kdc, vdc), (ks, vas, kds, vds)], diff_lambda, subln, i, lam_init,
                               n_batch=DEC_BATCH, t_q=DEC_SEQ, name=f"attn_l{l}_s")
            w_out, w_out_idx = e_w_out, i
        else:
            pp = _odd_proj(x, mod5, norm_g3, o_w_in, o_w_gate, o_b_gate, l, latent=False)
            lg_rows = jnp.repeat(log_gamma[i], DK_D, axis=-1).reshape(2, 1, H_D * DK_D)
            *outs_p, sg, sr = _scan(pp, lg_rows, wall, cst, bd, n_seq=BATCH, t_seq=SEQ, emit_state=True)
            new_sg.append(_unpair_states(sg))
            new_sr.append(_unpair_states(sr))
            mix_p = _odd_post(outs_p, pp, gla_n, ret_n, i)
            ps = _odd_proj(x, mod5, norm_g3, o_w_in, o_w_gate, o_b_gate, l, latent=True, tables=tab_r)
            s0 = (_pair_states(state_gla[:, i]), _pair_states(state_ret[:, i]))
            outs_s = _scan(ps, lg_rows, wall, cst, bd, n_seq=DEC_BATCH, t_seq=DEC_SEQ, s0=s0, emit_state=False)
            mix_s = _odd_post(outs_s, ps, gla_n, ret_n, i)
            w_out, w_out_idx = o_w_out, i
        x = _ffn(x, mod5, norm_g3, wg, wu, wd, l, 1, mix=(mix_p, mix_s), w_out=w_out, w_out_idx=w_out_idx,
                 final_g=final_g.reshape(1, D_MODEL) if l == DEPTH - 1 else None)

    y_p, y_s = x
    return (y_p.reshape(BATCH, SEQ, D_MODEL), y_s.reshape(DEC_BATCH, DEC_SEQ, D_MODEL),
            jnp.stack(new_ckv, axis=1), jnp.stack(new_krope, axis=1),
            jnp.stack(new_dk, axis=1), jnp.stack(new_dv, axis=1),
            jnp.stack(new_sg, axis=1), jnp.stack(new_sr, axis=1))
```

```python
import functools
import math

import numpy as np
import jax
import jax.numpy as jnp
from jax import lax
from jax.experimental import pallas as pl
from jax.experimental.pallas import tpu as pltpu

F32 = jnp.float32
BF16 = jnp.bfloat16

D_MODEL = 1024
BATCH = 16
SEQ = 256
DEPTH = 4
DEC_BATCH = 4
DEC_SEQ = 2048
PAST_LEN = 512
GRID_W = 64
N_EVEN = (DEPTH + 1) // 2
N_ODD = DEPTH // 2
N_MOD = 9
D_FF = 2816
H_A, NOPE_A, ROPE_A, V_A = 8, 64, 32, 64
Q_LORA, KV_LORA = 384, 256
H_B, DH_B = 4, 64
DV_B = 2 * DH_B
H_C, DK_C, DV_C = 4, 64, 128
GATE_RANK = 16
GATE_TAU = 16.0
H_D, DK_D, DV_D = 4, 64, 128
CHUNK = 64
ROPE_BASE = 10000.0
EPS = 1e-6

LANES = 128
SUBLANES = 8
VMEM_LIMIT_BYTES = 56 * 1024 * 1024

P_TOK = BATCH * SEQ
S_TOK = DEC_BATCH * DEC_SEQ
N_TOK = P_TOK + S_TOK
N_GROUPS = SUBLANES

TM = 512
TM_FFN = 512
FF_CHUNK = 256
TQ = 512
HEAD_W = LANES
LOG2E = math.log2(math.e)
EVEN_W = Q_LORA + KV_LORA + HEAD_W + 3 * H_B * 2 * DH_B
ODD_MAIN = 3072
ODD_W = ODD_MAIN + LANES
ODD_OUT = ODD_MAIN + 2 * H_C * DK_C
SCAN_SEQS = 4
LEVELS = (32, 16, 8, 4, 2, 1)
N_WALL = len(LEVELS) + 2


def _dot(a, b):
    return jnp.dot(a, b, preferred_element_type=F32)


def _dot_nt(a, b):
    return lax.dot_general(a, b, (((1,), (1,)), ((), ())), preferred_element_type=F32)


def _dot_tn(a, b):
    return lax.dot_general(a, b, (((0,), (0,)), ((), ())), preferred_element_type=F32)


def _silu(x):
    return x * (1.0 / (1.0 + jnp.exp(-x)))


def _rms(x, g):
    return x * lax.rsqrt(jnp.mean(x * x, axis=-1, keepdims=True) + EPS) * g


def _modulate(x, g, shift, scale):
    return _rms(x, g) * (1.0 + scale) + shift


def _group_of_block(i, tm):
    return jnp.where(i < P_TOK // tm, 0, 1 + (i - P_TOK // tm) // (DEC_SEQ // tm))


def _cparams(*sem):
    return pltpu.CompilerParams(dimension_semantics=sem, vmem_limit_bytes=VMEM_LIMIT_BYTES)


def _resident(shape, index_map):
    return pl.BlockSpec(shape, index_map, pipeline_mode=pl.Buffered(1))


def _adaln_kernel(c_ref, w_ref, b_ref, o_ref):
    s = _silu(c_ref[...]).astype(BF16)
    o_ref[...] = _dot(s, w_ref[...].astype(BF16)) + b_ref[...]


def _adaln(cond8, mod_w, mod_b):
    mod_b4 = mod_b.reshape(DEPTH, N_MOD, 1, D_MODEL)
    return pl.pallas_call(
        _adaln_kernel,
        grid=(DEPTH, N_MOD),
        in_specs=[
            pl.BlockSpec((N_GROUPS, D_MODEL), lambda l, j: (0, 0)),
            pl.BlockSpec((None, D_MODEL, D_MODEL), lambda l, j: (l, 0, j)),
            pl.BlockSpec((None, None, 1, D_MODEL), lambda l, j: (l, j, 0, 0)),
        ],
        out_specs=pl.BlockSpec((None, None, N_GROUPS, D_MODEL), lambda l, j: (l, j, 0, 0)),
        out_shape=jax.ShapeDtypeStruct((DEPTH, N_MOD, N_GROUPS, D_MODEL), F32),
        compiler_params=_cparams("arbitrary", "arbitrary"),
        name="adaln",
    )(cond8, mod_w, mod_b4)


def _mod_spec(l, j, grp):
    return pl.BlockSpec((None, None, None, 1, D_MODEL), lambda i: (l, j, grp(i), 0, 0))


def _split_specs(width):
    npb = P_TOK // TM_FFN
    return [pl.BlockSpec((TM_FFN, width), lambda i: (jnp.minimum(i, npb - 1), 0)),
            pl.BlockSpec((TM_FFN, width), lambda i: (jnp.maximum(i - npb, 0), 0))]


def _ffn_kernel(*refs, split_x, has_mix, final):
    it = iter(refs)
    is_ctx = pl.program_id(0) < P_TOK // TM_FFN
    pick = lambda p_ref, s_ref: jnp.where(is_ctx, p_ref[...], s_ref[...])
    x = pick(next(it), next(it)) if split_x else next(it)[...]
    if has_mix:
        mix = pick(next(it), next(it))
        wout_ref, gmix_ref = next(it), next(it)
    sh_ref, sc_ref, gt_ref, ng_ref = next(it), next(it), next(it), next(it)
    wg_ref, wu_ref, wd_ref = next(it), next(it), next(it)
    if final:
        fg_ref, yp_ref, ys_ref = next(it), next(it), next(it)
    else:
        o_ref = next(it)

    if has_mix:
        x = x + gmix_ref[...] * _dot(mix, wout_ref[...])
    h = _modulate(x, ng_ref[...], sh_ref[...], sc_ref[...]).astype(BF16)
    acc = jnp.zeros(x.shape, F32)
    for j in range(D_FF // FF_CHUNK):
        cols = slice(j * FF_CHUNK, (j + 1) * FF_CHUNK)
        a = _dot(h, wg_ref[:, cols].astype(BF16))
        u = _dot(h, wu_ref[:, cols].astype(BF16))
        acc = acc + _dot((_silu(a) * u).astype(BF16), wd_ref[cols, :].astype(BF16))
    out = x + 0.5 * gt_ref[...] * acc
    if final:
        y = _rms(out, fg_ref[...])

        @pl.when(is_ctx)
        def _():
            yp_ref[...] = y

        @pl.when(jnp.logical_not(is_ctx))
        def _():
            ys_ref[...] = y
    else:
        o_ref[...] = out


def _ffn(x, mod5, norm_g3, wg, wu, wd, l, k, *, mix=None, w_out=None, w_out_idx=None, final_g=None):
    j = 0 if k == 0 else 6
    grp = functools.partial(_group_of_block, tm=TM_FFN)
    row_spec = pl.BlockSpec((TM_FFN, D_MODEL), lambda i: (i, 0))
    split_x = isinstance(x, tuple)
    in_specs = _split_specs(D_MODEL) if split_x else [row_spec]
    args = list(x) if split_x else [x]
    if mix is not None:
        in_specs += _split_specs(D_MODEL) + [_resident((None, D_MODEL, D_MODEL), lambda i: (w_out_idx, 0, 0)),
                                             _mod_spec(l, 5, grp)]
        args += [*mix, w_out, mod5]
    in_specs += [_mod_spec(l, j, grp), _mod_spec(l, j + 1, grp), _mod_spec(l, j + 2, grp),
                 pl.BlockSpec((None, 1, D_MODEL), lambda i: (3 * l + (0 if k == 0 else 2), 0, 0)),
                 _resident((None, None, D_MODEL, D_FF), lambda i: (l, k, 0, 0)),
                 _resident((None, None, D_MODEL, D_FF), lambda i: (l, k, 0, 0)),
                 _resident((None, None, D_FF, D_MODEL), lambda i: (l, k, 0, 0))]
    args += [mod5, mod5, mod5, norm_g3, wg, wu, wd]
    if final_g is not None:
        in_specs.append(pl.BlockSpec((1, D_MODEL), lambda i: (0, 0)))
        args.append(final_g)
        out_specs = _split_specs(D_MODEL)
        out_shape = [jax.ShapeDtypeStruct((P_TOK, D_MODEL), F32), jax.ShapeDtypeStruct((S_TOK, D_MODEL), F32)]
    else:
        out_specs = row_spec
        out_shape = jax.ShapeDtypeStruct((N_TOK, D_MODEL), F32)
    return pl.pallas_call(
        functools.partial(_ffn_kernel, split_x=split_x, has_mix=mix is not None, final=final_g is not None),
        grid=(N_TOK // TM_FFN,),
        in_specs=in_specs,
        out_specs=out_specs,
        out_shape=out_shape,
        compiler_params=_cparams("arbitrary"),
        name=f"ffn_l{l}_{k}",
    )(*args)


def _rope_tables(subvectors, positions):
    t = next(iter(positions.values())).shape[0]
    cos = np.ones((t, LANES), np.float32)
    sin_m = np.zeros((t, LANES), np.float32)
    sin_p = np.zeros((t, LANES), np.float32)
    for first, n, key in subvectors:
        half = n // 2
        inv = np.float32(ROPE_BASE) ** (-np.arange(half, dtype=np.float32) * np.float32(2.0) / np.float32(n))
        ang = positions[key][:, None] * inv[None, :].astype(np.float32)
        c, s = np.cos(ang).astype(np.float32), np.sin(ang).astype(np.float32)
        cos[:, first:first + n] = np.concatenate([c, c], axis=1)
        sin_m[:, first:first + half] = -s
        sin_p[:, first + half:first + n] = s
    return jnp.asarray(cos), jnp.asarray(sin_m), jnp.asarray(sin_p)


def _tile_lanes(t, width):
    return t if width == LANES else jnp.concatenate([t] * (width // LANES), axis=1)


def _rope(y, cos, sin_m, sin_p, half):
    w = y.shape[1]
    c, sm, sp = (_tile_lanes(t, w) for t in (cos, sin_m, sin_p))
    return y * c + pltpu.roll(y, w - half, 1) * sm + pltpu.roll(y, half, 1) * sp


def _ones_lane(lane, width):
    idx = lax.broadcasted_iota(jnp.int32, (1, width), 1)
    return jnp.where((idx & (LANES - 1)) == lane, 1.0, 0.0)


def _even_proj_kernel(*refs, rope, cache):
    it = iter(refs)
    x_ref, sh_ref, sc_ref, ng_ref = next(it), next(it), next(it), next(it)
    win_ref, qn_ref, wuq_ref, kvn_ref, wk_ref, wv_ref = (next(it) for _ in range(6))
    if rope:
        tq = [next(it)[...] for _ in range(3)]
        td = [next(it)[...] for _ in range(3)]
    q_ref, k_ref, va_ref, qd_ref, kd_ref, vd_ref = (next(it) for _ in range(6))
    if cache:
        ckv_ref, krp_ref, kd32_ref, vd32_ref = (next(it) for _ in range(4))

    h = _modulate(x_ref[...], ng_ref[...], sh_ref[...], sc_ref[...]).astype(BF16)
    p = _dot(h, win_ref[...])
    o_ckv = Q_LORA
    o_krp = o_ckv + KV_LORA
    o_qd = o_krp + HEAD_W
    o_kd = o_qd + H_B * 2 * DH_B
    o_vd = o_kd + H_B * 2 * DH_B
    cqn = _rms(p[:, :Q_LORA], qn_ref[...]).astype(BF16)
    q = _dot(cqn, wuq_ref[...])
    ckvn = _rms(p[:, o_ckv:o_krp], kvn_ref[...])
    ckvb = ckvn.astype(BF16)
    krp = p[:, o_krp:o_qd]
    qd = p[:, o_qd:o_kd]
    kd = p[:, o_kd:o_vd]
    vd = p[:, o_vd:]
    if cache:
        ckv_ref[...] = ckvn
        krp_ref[...] = krp
        kd32_ref[...] = kd
        vd32_ref[...] = vd
    if rope:
        q = _rope(q, *tq, half=ROPE_A // 4)
        krp = _rope(krp, *tq, half=ROPE_A // 4)
        qd = _rope(qd, *td, half=DH_B // 4)
        kd = _rope(kd, *td, half=DH_B // 4)
    q_ref[...] = (q * (LOG2E * (NOPE_A + ROPE_A) ** -0.5)).astype(BF16)
    k_ref[...] = (_dot(ckvb, wk_ref[...]) + _tile_lanes(krp, H_A * HEAD_W)).astype(BF16)
    va = _dot(ckvb, wv_ref[...])
    if rope:
        va = va + _ones_lane(V_A, H_A * HEAD_W)
    va_ref[...] = va.astype(BF16)
    qd_ref[...] = (qd * (LOG2E * DH_B ** -0.5)).astype(BF16)
    kd_ref[...] = kd.astype(BF16)
    vd_ref[...] = vd.astype(BF16)


def _even_proj(x, mod5, norm_g3, w_in, q_norm, w_uq, kv_norm, w_k, w_v, l, *, latent, tables=None):
    i = l // 2
    n_rows = S_TOK if latent else P_TOK
    first = P_TOK // TM if latent else 0
    grp = (lambda b: 1 + b // (DEC_SEQ // TM)) if latent else (lambda b: 0)
    row_spec = lambda w: pl.BlockSpec((TM, w), lambda b: (b, 0))
    const = lambda shape: _resident((None,) + shape, lambda b: (i,) + (0,) * len(shape))
    in_specs = [pl.BlockSpec((TM, D_MODEL), lambda b: (first + b, 0)),
                _mod_spec(l, 3, grp), _mod_spec(l, 4, grp),
                pl.BlockSpec((None, 1, D_MODEL), lambda b: (3 * l + 1, 0, 0)),
                const((D_MODEL, EVEN_W)), const((1, Q_LORA)), const((Q_LORA, H_A * HEAD_W)),
                const((1, KV_LORA)), const((KV_LORA, H_A * HEAD_W)), const((KV_LORA, w_v.shape[-1]))]
    args = [x, mod5, mod5, norm_g3, w_in, q_norm, w_uq, kv_norm, w_k, w_v]
    if latent:
        in_specs += [pl.BlockSpec((TM, LANES), lambda b: (b % (DEC_SEQ // TM), 0))] * 6
        args += list(tables)
    widths = [H_A * HEAD_W, H_A * HEAD_W, w_v.shape[-1], H_B * DV_B, H_B * DV_B, H_B * DV_B]
    out_specs = [row_spec(w) for w in widths]
    out_shape = [jax.ShapeDtypeStruct((n_rows, w), BF16) for w in widths]
    if not latent:
        cache_w = [KV_LORA, HEAD_W, H_B * DV_B, H_B * DV_B]
        out_specs += [row_spec(w) for w in cache_w]
        out_shape += [jax.ShapeDtypeStruct((n_rows, w), F32) for w in cache_w]
    return pl.pallas_call(
        functools.partial(_even_proj_kernel, rope=latent, cache=not latent),
        grid=(n_rows // TM,),
        in_specs=in_specs,
        out_specs=out_specs,
        out_shape=out_shape,
        compiler_params=_cparams("parallel"),
        name=f"even_proj_l{l}_{'s' if latent else 'p'}",
    )(*args)


def _cache_kv_kernel(ckv_ref, krp_ref, wk_ref, wv_ref, k_ref, va_ref):
    c = ckv_ref[...].astype(BF16)
    k_ref[...] = (_dot(c, wk_ref[...]) + _tile_lanes(krp_ref[...], H_A * HEAD_W)).astype(BF16)
    va_ref[...] = (_dot(c, wv_ref[...]) + _ones_lane(V_A, H_A * HEAD_W)).astype(BF16)


def _cache_kv(ckv, krp, w_k, w_v, i):
    n = DEC_BATCH * PAST_LEN
    return pl.pallas_call(
        _cache_kv_kernel,
        grid=(n // PAST_LEN,),
        in_specs=[pl.BlockSpec((PAST_LEN, KV_LORA), lambda b: (b, 0)),
                  pl.BlockSpec((PAST_LEN, HEAD_W), lambda b: (b, 0)),
                  _resident((None, KV_LORA, H_A * HEAD_W), lambda b: (i, 0, 0)),
                  _resident((None, KV_LORA, H_A * HEAD_W), lambda b: (i, 0, 0))],
        out_specs=[pl.BlockSpec((PAST_LEN, H_A * HEAD_W), lambda b: (b, 0)),
                   pl.BlockSpec((PAST_LEN, H_A * HEAD_W), lambda b: (b, 0))],
        out_shape=[jax.ShapeDtypeStruct((n, H_A * HEAD_W), BF16),
                   jax.ShapeDtypeStruct((n, H_A * HEAD_W), BF16)],
        compiler_params=_cparams("parallel"),
        name=f"cache_kv_{i}",
    )(ckv, krp, w_k, w_v)


def _exp2_parts(s):
    m = functools.reduce(jnp.maximum, [jnp.max(si, axis=-1, keepdims=True) for si in s])
    return [jnp.exp2(si - m) for si in s]


def _rowsum(ps):
    return functools.reduce(jnp.add, [jnp.sum(p, axis=-1, keepdims=True) for p in ps])


def _pv(ps, vs):
    return functools.reduce(jnp.add, [_dot(p.astype(BF16), v) for p, v in zip(ps, vs)])


def _attn_kernel(*refs, lam_init, n_seg, ones_col):
    q_ref, qd_ref = refs[:2]
    segs = [refs[2 + 4 * g:6 + 4 * g] for g in range(n_seg)]
    lam_ref, sub_ref, o_ref = refs[2 + 4 * n_seg:]
    lp = lam_ref[...]
    lam = (jnp.exp(jnp.sum(lp[0:1] * lp[1:2], axis=-1, keepdims=True))
           - jnp.exp(jnp.sum(lp[2:3] * lp[3:4], axis=-1, keepdims=True)) + lam_init)
    first = lax.broadcasted_iota(jnp.int32, (1, DV_B), 1) < DH_B
    va_w = HEAD_W if ones_col else V_A

    def scores(task):
        kind, h = task
        if kind == "mla":
            cols = slice(h * HEAD_W, (h + 1) * HEAD_W)
            return ([_dot_nt(q_ref[:, cols], sg[0][:, cols]) for sg in segs],)
        cols = slice(h * DV_B, (h + 1) * DV_B)
        qh = qd_ref[:, cols].astype(F32)
        q1 = jnp.where(first, qh, 0.0).astype(BF16)
        q2 = jnp.where(first, 0.0, qh).astype(BF16)
        return tuple([_dot_nt(qx, sg[2][:, cols]) for sg in segs] for qx in (q1, q2))

    tasks = [("mla", h) for h in range(H_A)] + [("diff", h) for h in range(H_B)]
    outs = []
    ahead = scores(tasks[0])
    for t, (kind, h) in enumerate(tasks):
        cur = ahead
        if t + 1 < len(tasks):
            ahead = scores(tasks[t + 1])
        if kind == "mla":
            p = _exp2_parts(cur[0])
            acc = _pv(p, [sg[1][:, h * va_w:(h + 1) * va_w] for sg in segs])
            inv = 1.0 / (acc[:, V_A:V_A + 1] if ones_col else _rowsum(p))
            outs.append(acc[:, :V_A] * inv)
        else:
            p1, p2 = _exp2_parts(cur[0]), _exp2_parts(cur[1])
            w1, w2 = 1.0 / _rowsum(p1), lam / _rowsum(p2)
            o = _pv([a * w1 - b * w2 for a, b in zip(p1, p2)],
                    [sg[3][:, h * DV_B:(h + 1) * DV_B] for sg in segs])
            outs.append(_rms(o, sub_ref[...]) * (1.0 - lam_init))
    o_ref[...] = jnp.concatenate(outs, axis=1).astype(BF16)


def _attention(q, qd, segs, lam_p, subln, i, lam_init, *, n_batch, t_q, name):
    tq = min(TQ, t_q)
    nq = t_q // tq
    q_spec = lambda w: pl.BlockSpec((tq, w), lambda b, j: (b * nq + j, 0))
    in_specs = [q_spec(H_A * HEAD_W), q_spec(H_B * DV_B)]
    args = [q, qd]
    for seg in segs:
        t_seg = seg[0].shape[0] // n_batch
        for a in seg:
            in_specs.append(pl.BlockSpec((t_seg, a.shape[1]), lambda b, j: (b, 0), pipeline_mode=pl.Buffered(1)))
            args.append(a)
    in_specs += [pl.BlockSpec((None, 4, DH_B), lambda b, j: (i, 0, 0)),
                 pl.BlockSpec((None, 1, DV_B), lambda b, j: (i, 0, 0))]
    args += [lam_p, subln]
    ones_col = segs[0][1].shape[1] == H_A * HEAD_W
    return pl.pallas_call(
        functools.partial(_attn_kernel, lam_init=lam_init, n_seg=len(segs), ones_col=ones_col),
        grid=(n_batch, nq),
        in_specs=in_specs,
        out_specs=q_spec(D_MODEL),
        out_shape=jax.ShapeDtypeStruct((n_batch * t_q, D_MODEL), BF16),
        compiler_params=_cparams("parallel", "arbitrary"),
        name=name,
    )(*args)


def _odd_proj_kernel(*refs, rope):
    it = iter(refs)
    x_ref, sh_ref, sc_ref, ng_ref, win_ref, wgate_ref, bgate_ref = (next(it) for _ in range(7))
    if rope:
        tr = [next(it)[...] for _ in range(3)]
    o_ref = next(it)

    h = _modulate(x_ref[...], ng_ref[...], sh_ref[...], sc_ref[...]).astype(BF16)
    p = _dot(h, win_ref[...])
    o_ref[:, 0:256] = p[:, 0:256] * (DK_C ** -0.5)
    o_ref[:, 256:1536] = p[:, 256:1536]
    rq = p[:, 1536:1792]
    rk = p[:, 1792:2048] * (DK_D ** -0.5)
    if rope:
        rq = _rope(rq, *tr, half=DK_D // 2)
        rk = _rope(rk, *tr, half=DK_D // 2)
    o_ref[:, 1536:1792] = rq
    o_ref[:, 1792:2048] = rk
    o_ref[:, 2048:ODD_MAIN] = p[:, 2048:ODD_MAIN]
    logit = _dot(p[:, ODD_MAIN:].astype(BF16), wgate_ref[...]) + bgate_ref[...]
    o_ref[:, ODD_MAIN:] = (jnp.minimum(logit, 0.0) - jnp.log1p(jnp.exp(-jnp.abs(logit)))) * (1.0 / GATE_TAU)


def _odd_proj(x, mod5, norm_g3, w_in, w_gate, b_gate, l, *, latent, tables=None):
    i = l // 2
    n_rows = S_TOK if latent else P_TOK
    first = P_TOK // TM if latent else 0
    grp = (lambda b: 1 + b // (DEC_SEQ // TM)) if latent else (lambda b: 0)
    in_specs = [pl.BlockSpec((TM, D_MODEL), lambda b: (first + b, 0)),
                _mod_spec(l, 3, grp), _mod_spec(l, 4, grp),
                pl.BlockSpec((None, 1, D_MODEL), lambda b: (3 * l + 1, 0, 0)),
                _resident((None, D_MODEL, ODD_W), lambda b: (i, 0, 0)),
                _resident((None, LANES, 2 * H_C * DK_C), lambda b: (i, 0, 0)),
                _resident((None, 1, 2 * H_C * DK_C), lambda b: (i, 0, 0))]
    args = [x, mod5, mod5, norm_g3, w_in, w_gate, b_gate]
    if latent:
        in_specs += [pl.BlockSpec((TM, LANES), lambda b: (b % (DEC_SEQ // TM), 0))] * 3
        args += list(tables)
    return pl.pallas_call(
        functools.partial(_odd_proj_kernel, rope=latent),
        grid=(n_rows // TM,),
        in_specs=in_specs,
        out_specs=pl.BlockSpec((TM, ODD_OUT), lambda b: (b, 0)),
        out_shape=jax.ShapeDtypeStruct((n_rows, ODD_OUT), F32),
        compiler_params=_cparams("parallel"),
        name=f"odd_proj_l{l}_{'s' if latent else 'p'}",
    )(*args)


def _scan_weights():
    c = CHUNK
    t = np.arange(c)
    out = np.zeros((2, N_WALL * c, c), np.float32)
    for d in range(2):
        run = (t[None, :] <= t[:, None]) if d == 0 else (t[None, :] >= t[:, None])
        run = run.astype(np.float32)
        out[d, 0:c] = run
        for li, b in enumerate(LEVELS):
            base = (t // (2 * b)) * (2 * b)
            r = base + (b - 1 if d == 0 else b)
            out[d, (li + 1) * c:(li + 2) * c] = run - run[r]
        out[d, (N_WALL - 1) * c:] = 1.0 - run
    return out


C_SAME = 0
C_QM = C_SAME + len(LEVELS) + 1
C_DIST = C_QM + 3 * 2 * len(LEVELS)
C_CAUSAL = C_DIST + 1
C_CNT = C_CAUSAL + 2
N_CONST = C_CNT + 2


def _scan_consts():
    c = CHUNK
    t = np.arange(c)[:, None]
    s = np.arange(2 * c)[None, :] % c
    head0 = (np.arange(2 * c)[None, :] < c)
    out = np.zeros((N_CONST, c, 2 * c), np.float32)
    for li, b in enumerate(LEVELS):
        out[C_SAME + li] = (t // (2 * b)) == (s // (2 * b))
        later = (t % (2 * b)) >= b
        for d in range(2):
            is_q = later if d == 0 else ~later
            base = C_QM + 3 * (d * len(LEVELS) + li)
            out[base] = is_q
            out[base + 1] = ~is_q & head0
            out[base + 2] = ~is_q & ~head0
    out[C_SAME + len(LEVELS)] = (t == s)
    out[C_DIST] = np.abs(t - s)
    out[C_CAUSAL] = s <= t
    out[C_CAUSAL + 1] = s >= t
    out[C_CNT] = t + 1
    out[C_CNT + 1] = c - t
    return out


def _pair_rhs(kf, m0, m1):
    return jnp.concatenate([kf * m0, kf * m1], axis=0).astype(BF16)


def _pair_values(v):
    left = lax.broadcasted_iota(jnp.int32, (1, v.shape[1]), 1) < v.shape[1] // 2
    return jnp.concatenate([jnp.where(left, v, 0.0), jnp.where(left, 0.0, v)], axis=0).astype(BF16)


def _scan_kernel(*refs, n_chunks, n_seq, has_s0, emit_state):
    it = iter(refs)
    ins = [[next(it) for _ in range(7)] for _ in range(2)]
    wall_ref, cst_ref, bd_ref, lg_ref = next(it), next(it), next(it), next(it)
    if has_s0:
        s0g_ref, s0r_ref = next(it), next(it)
    o_refs = [[next(it), next(it)] for _ in range(2)]
    if emit_state:
        sgo_ref, sro_ref = next(it), next(it)
    sg_ref, sr_ref = next(it), next(it)
    c_id = pl.program_id(1)

    @pl.when(c_id == 0)
    def _():
        if has_s0:
            sg_ref[...] = s0g_ref[...]
            sr_ref[...] = s0r_ref[...]
        else:
            sg_ref[...] = jnp.zeros(sg_ref.shape, F32)
            sr_ref[...] = jnp.zeros(sr_ref.shape, F32)

    n_lv = len(LEVELS)
    lane = lax.broadcasted_iota(jnp.int32, (1, 2 * DK_C), 1)
    head0 = jnp.where(lane < DK_C, 1.0, 0.0)
    head1 = 1.0 - head0
    bd = bd_ref[...]
    streams = [(sq, d) for sq in range(n_seq) for d in range(2)]
    dall = {}
    for sq, d in streams:
        g = ins[d][3][sq]
        g_hi = g.astype(BF16)
        g_lo = (g - g_hi.astype(F32)).astype(BF16)
        dall[sq, d] = _dot(wall_ref[d], jnp.concatenate([g_hi, g_lo], axis=0))
    outs = {s: [] for s in streams}
    for p in range(H_C // 2):
        ln = slice(p * 2 * DK_C, (p + 1) * 2 * DK_C)
        qk = {(sq, d): (ins[d][0][sq, :, ln], ins[d][1][sq, :, ln]) for sq, d in streams}
        a = {s: _dot_nt(qk[s][0].astype(BF16), _pair_rhs(qk[s][1], head0, head1)) * cst_ref[C_SAME + n_lv]
             for s in streams}
        for li in range(n_lv):
            for s in streams:
                q, k = qk[s]
                e = jnp.exp(-jnp.abs(dall[s][(li + 1) * CHUNK:(li + 2) * CHUNK, ln]))
                base = C_QM + 3 * (s[1] * n_lv + li)
                a[s] = a[s] + _dot_nt((q * (e * cst_ref[base])).astype(BF16),
                                      _pair_rhs(k * e, cst_ref[base + 1], cst_ref[base + 2])) * cst_ref[C_SAME + li]
        for s in streams:
            sq, d = s
            q, k = qk[s]
            bc = dall[s][0:CHUNK, ln]
            tot = bc[CHUNK - 1:CHUNK] if d == 0 else bc[0:1]
            qb = (q * jnp.exp(bc)).astype(BF16)
            kb = (k * jnp.exp(dall[s][(N_WALL - 1) * CHUNK:, ln])).astype(BF16)
            v = ins[d][2][sq, :, p * 2 * DV_C:(p + 1) * 2 * DV_C]
            st = sg_ref[sq, d, p]
            outs[s].append(_dot(a[s].astype(BF16), _pair_values(v)) + _dot_nt(qb, st.astype(BF16)))
            sg_ref[sq, d, p] = st * jnp.exp(tot) + _dot_tn(v.astype(BF16), kb) * bd
    for sq, d in streams:
        o_refs[d][0][sq] = jnp.concatenate(outs[sq, d], axis=1)
    outs = {s: [] for s in streams}
    for p in range(H_D // 2):
        ln = slice(p * 2 * DK_D, (p + 1) * 2 * DK_D)
        for d in range(2):
            lg = lg_ref[d, :, ln]
            cnt = cst_ref[C_CNT + d]
            dec = jnp.exp(lg * cst_ref[C_DIST]) * cst_ref[C_CAUSAL + d]
            q_dec, k_dec, s_dec = jnp.exp(lg * cnt), jnp.exp(lg * (CHUNK - cnt)), jnp.exp(lg * CHUNK)
            for sq in range(n_seq):
                q, k = ins[d][4][sq, :, ln], ins[d][5][sq, :, ln]
                a = _dot_nt(q.astype(BF16), _pair_rhs(k, head0, head1)) * dec
                v = ins[d][6][sq, :, p * 2 * DV_D:(p + 1) * 2 * DV_D]
                st = sr_ref[sq, d, p]
                outs[sq, d].append(_dot(a.astype(BF16), _pair_values(v)) + _dot_nt((q * q_dec).astype(BF16), st.astype(BF16)))
                sr_ref[sq, d, p] = st * s_dec + _dot_tn(v.astype(BF16), (k * k_dec).astype(BF16)) * bd
    for sq, d in streams:
        o_refs[d][1][sq] = jnp.concatenate(outs[sq, d], axis=1)

    if emit_state:
        @pl.when(c_id == n_chunks - 1)
        def _():
            sgo_ref[...] = sg_ref[...]
            sro_ref[...] = sr_ref[...]


def _pair_states(s):
    n, _, h, dk, dv = s.shape
    st = jnp.swapaxes(s, -1, -2).reshape(n, 2, h // 2, 2, dv, dk)
    z = jnp.zeros_like(st[:, :, :, 0])
    top = jnp.concatenate([st[:, :, :, 0], z], axis=-1)
    bot = jnp.concatenate([z, st[:, :, :, 1]], axis=-1)
    return jnp.concatenate([top, bot], axis=-2)


def _unpair_states(sp):
    n, _, hp, dv2, dk2 = sp.shape
    dv, dk = dv2 // 2, dk2 // 2
    st = jnp.stack([sp[..., :dv, :dk], sp[..., dv:, dk:]], axis=3).reshape(n, 2, 2 * hp, dv, dk)
    return jnp.swapaxes(st, -1, -2)


def _scan(p, lg_rows, wall, cst, bd, *, n_seq, t_seq, s0=None, emit_state):
    nc = t_seq // CHUNK
    ns = SCAN_SEQS
    p3 = p.reshape(n_seq, t_seq, ODD_OUT)
    fwd = lambda blk: (lambda b, c: (b, c, blk))
    bwd = lambda blk: (lambda b, c: (b, nc - 1 - c, blk))
    in_specs, args = [], []
    for m in (fwd, bwd):
        d = 0 if m is fwd else 1
        in_specs += [pl.BlockSpec((ns, CHUNK, 256), m(0)), pl.BlockSpec((ns, CHUNK, 256), m(1)),
                     pl.BlockSpec((ns, CHUNK, 512), m(1)), pl.BlockSpec((ns, CHUNK, 256), m(ODD_MAIN // 256 + d)),
                     pl.BlockSpec((ns, CHUNK, 256), m(6)), pl.BlockSpec((ns, CHUNK, 256), m(7)),
                     pl.BlockSpec((ns, CHUNK, 512), m(4))]
        args += [p3] * 7
    in_specs += [_resident(wall.shape, lambda b, c: (0, 0, 0)), _resident(cst.shape, lambda b, c: (0, 0, 0)),
                 _resident(bd.shape, lambda b, c: (0, 0)), _resident(lg_rows.shape, lambda b, c: (0, 0, 0))]
    args += [wall, cst, bd, lg_rows]
    st_shape = (ns, 2, H_C // 2, 2 * DV_C, 2 * DK_C)
    st_spec = pl.BlockSpec(st_shape, lambda b, c: (b, 0, 0, 0, 0))
    if s0 is not None:
        in_specs += [st_spec, st_spec]
        args += list(s0)
    out_specs = [pl.BlockSpec((ns, CHUNK, 512), fwd(0)), pl.BlockSpec((ns, CHUNK, 512), fwd(0)),
                 pl.BlockSpec((ns, CHUNK, 512), bwd(0)), pl.BlockSpec((ns, CHUNK, 512), bwd(0))]
    out_shape = [jax.ShapeDtypeStruct((n_seq, t_seq, 512), F32)] * 4
    if emit_state:
        out_specs += [st_spec, st_spec]
        out_shape += [jax.ShapeDtypeStruct((n_seq,) + st_shape[1:], F32)] * 2
    res = pl.pallas_call(
        functools.partial(_scan_kernel, n_chunks=nc, n_seq=ns, has_s0=s0 is not None, emit_state=emit_state),
        grid=(n_seq // ns, nc),
        in_specs=in_specs,
        out_specs=out_specs,
        out_shape=out_shape,
        scratch_shapes=[pltpu.VMEM(st_shape, F32), pltpu.VMEM(st_shape, F32)],
        compiler_params=_cparams("parallel", "arbitrary"),
        name=f"scan_{'s' if s0 is not None else 'p'}",
    )(*args)
    return [r.reshape(n_seq * t_seq, 512) for r in res[:4]] + list(res[4:])


def _odd_post_kernel(ogf_ref, orf_ref, ogb_ref, orb_ref, gr_ref, rr_ref, gn_ref, rn_ref, o_ref):
    og = ogf_ref[...] + ogb_ref[...]
    orr = orf_ref[...] + orb_ref[...]
    gr, rr = gr_ref[...], rr_ref[...]
    outs = []
    for h in range(H_C):
        hc = slice(h * DV_C, (h + 1) * DV_C)
        outs.append(_rms(og[:, hc], gn_ref[...]) * _silu(gr[:, hc]))
    for h in range(H_D):
        hc = slice(h * DV_D, (h + 1) * DV_D)
        xh = orr[:, hc]
        mu = jnp.mean(xh, axis=-1, keepdims=True)
        xc = xh - mu
        var = jnp.mean(xc * xc, axis=-1, keepdims=True)
        outs.append(xc * lax.rsqrt(var + EPS) * rn_ref[...] * _silu(rr[:, hc]))
    o_ref[...] = jnp.concatenate(outs, axis=1).astype(BF16)


def _odd_post(scan_outs, p, gla_norm, ret_norm, i):
    n = p.shape[0]
    row = lambda blk: pl.BlockSpec((TM, 512), lambda b: (b, blk))
    return pl.pallas_call(
        _odd_post_kernel,
        grid=(n // TM,),
        in_specs=[row(0)] * 4 + [row(2), row(5),
                                 pl.BlockSpec((None, 1, DV_C), lambda b: (i, 0, 0)),
                                 pl.BlockSpec((None, 1, DV_D), lambda b: (i, 0, 0))],
        out_specs=pl.BlockSpec((TM, D_MODEL), lambda b: (b, 0)),
        out_shape=jax.ShapeDtypeStruct((n, D_MODEL), BF16),
        compiler_params=_cparams("parallel"),
        name="odd_post",
    )(*scan_outs, p, p, gla_norm, ret_norm)


def _pad_cols(w, width):
    return jnp.pad(w, [(0, 0)] * (w.ndim - 1) + [(0, width - w.shape[-1])])


def _even_weights(even_w_in, mla_w_uq, mla_w_ukv):
    cq, ckv, kr, qd, kd, vd = jnp.split(even_w_in, np.cumsum([Q_LORA, KV_LORA, ROPE_A, 512, 512]).tolist(), axis=-1)
    zeros = lambda n: jnp.zeros(even_w_in.shape[:-1] + (n,), even_w_in.dtype)
    w_in = jnp.concatenate([cq, ckv, zeros(NOPE_A), kr, zeros(HEAD_W - NOPE_A - ROPE_A), qd, kd, vd], axis=-1)
    uq = mla_w_uq.reshape(N_EVEN, Q_LORA, H_A, NOPE_A + ROPE_A)
    w_uq = _pad_cols(uq, HEAD_W).reshape(N_EVEN, Q_LORA, H_A * HEAD_W)
    ukv = mla_w_ukv.reshape(N_EVEN, KV_LORA, H_A, NOPE_A + V_A)
    w_k = _pad_cols(ukv[..., :NOPE_A], HEAD_W).reshape(N_EVEN, KV_LORA, H_A * HEAD_W)
    w_v = ukv[..., NOPE_A:].reshape(N_EVEN, KV_LORA, H_A * V_A)
    w_va = _pad_cols(ukv[..., NOPE_A:], HEAD_W).reshape(N_EVEN, KV_LORA, H_A * HEAD_W)
    return tuple(w.astype(BF16) for w in (w_in, w_uq, w_k, w_v, w_va))


def _odd_weights(odd_w_in, gla_w_gate, gla_b_gate):
    cuts = np.cumsum([256, 256, 512, 2 * GATE_RANK, 512, 256, 256, 512]).tolist()
    gq, gk, gv, glr, gr, rq, rk, rv, rr = jnp.split(odd_w_in, cuts, axis=-1)
    w_in = jnp.concatenate([gq, gk, gv, gr, rq, rk, rv, rr, _pad_cols(glr, LANES)], axis=-1)
    w_gate = jnp.zeros((N_ODD, LANES, 2 * H_C * DK_C), F32)
    for d in range(2):
        w_gate = w_gate.at[:, d * GATE_RANK:(d + 1) * GATE_RANK, d * 256:(d + 1) * 256].set(gla_w_gate[:, d])
    b_gate = gla_b_gate.reshape(N_ODD, 1, 2 * H_C * DK_C)
    return w_in.astype(BF16), w_gate.astype(BF16), b_gate


def kernel(x_prompt, x_sample, cache_mla_ckv, cache_mla_krope, cache_diff_k, cache_diff_v, state_gla, state_ret, c, c_ctx, mod_w, mod_b, norm_g, ffn_w_gate, ffn_w_up, ffn_w_down, even_w_in, mla_q_norm, mla_w_uq, mla_kv_norm, mla_w_ukv, diff_lambda, diff_subln, even_w_out, odd_w_in, gla_w_gate, gla_b_gate, gla_norm, ret_decay, ret_norm, odd_w_out, final_g):
    x = (x_prompt.reshape(P_TOK, D_MODEL), x_sample.reshape(S_TOK, D_MODEL))
    cond8 =jnp.concatenate([c_ctx[None], c, jnp.zeros((N_GROUPS - 1 - DEC_BATCH, D_MODEL), F32)], axis=0)
    mod5 = _adaln(cond8, mod_w, mod_b).reshape(DEPTH, N_MOD, N_GROUPS, 1, D_MODEL)
    norm_g3 = norm_g.reshape(DEPTH * 3, 1, D_MODEL)
    wg, wu, wd = ffn_w_gate, ffn_w_up, ffn_w_down
    e_w_in, e_w_uq, e_w_k, e_w_v, e_w_va = _even_weights(even_w_in, mla_w_uq, mla_w_ukv)
    o_w_in, o_w_gate, o_b_gate = _odd_weights(odd_w_in, gla_w_gate, gla_b_gate)
    e_w_out, o_w_out = even_w_out.astype(BF16), odd_w_out.astype(BF16)
    q_norm = mla_q_norm.reshape(N_EVEN, 1, Q_LORA)
    kv_norm = mla_kv_norm.reshape(N_EVEN, 1, KV_LORA)
    subln = diff_subln.reshape(N_EVEN, 1, DV_B)
    gla_n = gla_norm.reshape(N_ODD, 1, DV_C)
    ret_n = ret_norm.reshape(N_ODD, 1, DV_D)
    log_gamma = jnp.log1p(-jnp.exp2(-ret_decay))
    wall = jnp.asarray(np.tile(_scan_weights(), (1, 1, 2)), BF16)
    cst = jnp.asarray(_scan_consts())
    bd = jnp.asarray(np.kron(np.eye(2, dtype=np.float32), np.ones((DV_C, DK_C), np.float32)))

    tpos = np.arange(DEC_SEQ, dtype=np.float32)
    pos = {"row": np.floor(tpos / GRID_W), "col": tpos % GRID_W, "t": tpos}
    tab_q = _rope_tables([(NOPE_A, ROPE_A // 2, "row"), (NOPE_A + ROPE_A // 2, ROPE_A // 2, "col")], pos)
    tab_d = _rope_tables([(o, DH_B // 2, key) for o, key in ((0, "row"), (32, "col"), (64, "row"), (96, "col"))], pos)
    tab_r = _rope_tables([(0, DK_D, "t"), (DK_D, DK_D, "t")], pos)

    mix = None
    w_out, w_out_idx = None, None
    new_ckv, new_krope, new_dk, new_dv, new_sg, new_sr = [], [], [], [], [], []
    for l in range(DEPTH):
        i = l // 2
        x = _ffn(x, mod5, norm_g3, wg, wu, wd, l, 0)
        if l % 2 == 0:
            lam_init = 0.8 - 0.6 * math.exp(-0.3 * l)
            qp, kp, vap, qdp, kdp, vdp, ckv, krp, kd32, vd32 = _even_proj(
                x, mod5, norm_g3, e_w_in, q_norm, e_w_uq, kv_norm, e_w_k, e_w_v, l, latent=False)
            new_ckv.append(ckv.reshape(BATCH, SEQ, KV_LORA))
            new_krope.append(krp[:, NOPE_A:NOPE_A + ROPE_A].reshape(BATCH, SEQ, ROPE_A))
            new_dk.append(kd32.reshape(BATCH, SEQ, H_B, 2 * DH_B))
            new_dv.append(vd32.reshape(BATCH, SEQ, H_B, DV_B))
            mix_p = _attention(qp, qdp, [(kp, vap, kdp, vdp)],
                               diff_lambda, subln, i, lam_init, n_batch=BATCH, t_q=SEQ, name=f"attn_l{l}_p")
            qs, ks, vas, qds, kds, vds = _even_proj(
                x, mod5, norm_g3, e_w_in, q_norm, e_w_uq, kv_norm, e_w_k, e_w_va, l, latent=True,
                tables=tab_q + tab_d)
            krp_c = jnp.pad(cache_mla_krope[:, i].reshape(DEC_BATCH * PAST_LEN, ROPE_A),
                            ((0, 0), (NOPE_A, HEAD_W - NOPE_A - ROPE_A)))
            kc, vac = _cache_kv(cache_mla_ckv[:, i].reshape(DEC_BATCH * PAST_LEN, KV_LORA), krp_c, e_w_k, e_w_va, i)
            n_c = DEC_BATCH * PAST_LEN
            kdc = cache_diff_k[:, i].reshape(n_c, H_B * DV_B).astype(BF16)
            vdc = cache_diff_v[:, i].reshape(n_c, H_B * DV_B).astype(BF16)
            mix_s = _attention(qs, qds, [(kc, vac, kdc, vdc), (ks, vas, kds, vds)], diff_lambda, subln, i, lam_init,
                               n_batch=DEC_BATCH, t_q=DEC_SEQ, name=f"attn_l{l}_s")
            w_out, w_out_idx = e_w_out, i
        else:
            pp = _odd_proj(x, mod5, norm_g3, o_w_in, o_w_gate, o_b_gate, l, latent=False)
            lg_rows = jnp.repeat(log_gamma[i], DK_D, axis=-1).reshape(2, 1, H_D * DK_D)
            *outs_p, sg, sr = _scan(pp, lg_rows, wall, cst, bd, n_seq=BATCH, t_seq=SEQ, emit_state=True)
            new_sg.append(_unpair_states(sg))
            new_sr.append(_unpair_states(sr))
            mix_p = _odd_post(outs_p, pp, gla_n, ret_n, i)
            ps = _odd_proj(x, mod5, norm_g3, o_w_in, o_w_gate, o_b_gate, l, latent=True, tables=tab_r)
            s0 = (_pair_states(state_gla[:, i]), _pair_states(state_ret[:, i]))
            outs_s = _scan(ps, lg_rows, wall, cst, bd, n_seq=DEC_BATCH, t_seq=DEC_SEQ, s0=s0, emit_state=False)
            mix_s = _odd_post(outs_s, ps, gla_n, ret_n, i)
            w_out, w_out_idx = o_w_out, i
        x = _ffn(x, mod5, norm_g3, wg, wu, wd, l, 1, mix=(mix_p, mix_s), w_out=w_out, w_out_idx=w_out_idx,
                 final_g=final_g.reshape(1, D_MODEL) if l == DEPTH - 1 else None)

    y_p, y_s = x
    return (y_p.reshape(BATCH, SEQ, D_MODEL), y_s.reshape(DEC_BATCH, DEC_SEQ, D_MODEL),
            jnp.stack(new_ckv, axis=1), jnp.stack(new_krope, axis=1),
            jnp.stack(new_dk, axis=1), jnp.stack(new_dv, axis=1),
            jnp.stack(new_sg, axis=1), jnp.stack(new_sr, axis=1))
```

```python
import functools
import math

import numpy as np
import jax
import jax.numpy as jnp
from jax import lax
from jax.experimental import pallas as pl
from jax.experimental.pallas import tpu as pltpu

F32 = jnp.float32
BF16 = jnp.bfloat16

D_MODEL = 1024
BATCH = 16
SEQ = 256
DEPTH = 4
DEC_BATCH = 4
DEC_SEQ = 2048
PAST_LEN = 512
GRID_W = 64
N_EVEN = (DEPTH + 1) // 2
N_ODD = DEPTH // 2
N_MOD = 9
D_FF = 2816
H_A, NOPE_A, ROPE_A, V_A = 8, 64, 32, 64
Q_LORA, KV_LORA = 384, 256
H_B, DH_B = 4, 64
DV_B = 2 * DH_B
H_C, DK_C, DV_C = 4, 64, 128
GATE_RANK = 16
GATE_TAU = 16.0
H_D, DK_D, DV_D = 4, 64, 128
CHUNK = 64
ROPE_BASE = 10000.0
EPS = 1e-6

LANES = 128
SUBLANES = 8
VMEM_LIMIT_BYTES = 56 * 1024 * 1024

P_TOK = BATCH * SEQ
S_TOK = DEC_BATCH * DEC_SEQ
N_TOK = P_TOK + S_TOK
N_GROUPS = SUBLANES

TM = 512
TM_FFN = 512
FF_CHUNK = 256
TQ = 512
HEAD_W = LANES
LOG2E = math.log2(math.e)
EVEN_W = Q_LORA + KV_LORA + HEAD_W + 3 * H_B * 2 * DH_B
ODD_MAIN = 3072
ODD_W = ODD_MAIN + LANES
ODD_OUT = ODD_MAIN + 2 * H_C * DK_C
SCAN_SEQS = 4
LEVELS = (32, 16, 8, 4, 2, 1)
N_WALL = len(LEVELS) + 2


def _dot(a, b):
    return jnp.dot(a, b, preferred_element_type=F32)


def _dot_nt(a, b):
    return lax.dot_general(a, b, (((1,), (1,)), ((), ())), preferred_element_type=F32)


def _dot_tn(a, b):
    return lax.dot_general(a, b, (((0,), (0,)), ((), ())), preferred_element_type=F32)


def _silu(x):
    return x * (1.0 / (1.0 + jnp.exp(-x)))


def _rms(x, g):
    return x * lax.rsqrt(jnp.mean(x * x, axis=-1, keepdims=True) + EPS) * g


def _modulate(x, g, shift, scale):
    return _rms(x, g) * (1.0 + scale) + shift


def _group_of_block(i, tm):
    return jnp.where(i < P_TOK // tm, 0, 1 + (i - P_TOK // tm) // (DEC_SEQ // tm))


def _cparams(*sem):
    return pltpu.CompilerParams(dimension_semantics=sem, vmem_limit_bytes=VMEM_LIMIT_BYTES)


def _resident(shape, index_map):
    return pl.BlockSpec(shape, index_map, pipeline_mode=pl.Buffered(1))


def _adaln_kernel(c_ref, w_ref, b_ref, o_ref):
    s = _silu(c_ref[...]).astype(BF16)
    o_ref[...] = _dot(s, w_ref[...].astype(BF16)) + b_ref[...]


def _adaln(cond8, mod_w, mod_b):
    mod_b4 = mod_b.reshape(DEPTH, N_MOD, 1, D_MODEL)
    return pl.pallas_call(
        _adaln_kernel,
        grid=(DEPTH, N_MOD),
        in_specs=[
            pl.BlockSpec((N_GROUPS, D_MODEL), lambda l, j: (0, 0)),
            pl.BlockSpec((None, D_MODEL, D_MODEL), lambda l, j: (l, 0, j)),
            pl.BlockSpec((None, None, 1, D_MODEL), lambda l, j: (l, j, 0, 0)),
        ],
        out_specs=pl.BlockSpec((None, None, N_GROUPS, D_MODEL), lambda l, j: (l, j, 0, 0)),
        out_shape=jax.ShapeDtypeStruct((DEPTH, N_MOD, N_GROUPS, D_MODEL), F32),
        compiler_params=_cparams("arbitrary", "arbitrary"),
        name="adaln",
    )(cond8, mod_w, mod_b4)


def _mod_spec(l, j, grp):
    return pl.BlockSpec((None, None, None, 1, D_MODEL), lambda i: (l, j, grp(i), 0, 0))


def _split_specs(width):
    npb = P_TOK // TM_FFN
    return [pl.BlockSpec((TM_FFN, width), lambda i: (jnp.minimum(i, npb - 1), 0)),
            pl.BlockSpec((TM_FFN, width), lambda i: (jnp.maximum(i - npb, 0), 0))]


def _ffn_kernel(*refs, split_x, has_mix, final):
    it = iter(refs)
    is_ctx = pl.program_id(0) < P_TOK // TM_FFN
    pick = lambda p_ref, s_ref: jnp.where(is_ctx, p_ref[...], s_ref[...])
    x = pick(next(it), next(it)) if split_x else next(it)[...]
    if has_mix:
        mix = pick(next(it), next(it))
        wout_ref, gmix_ref = next(it), next(it)
    sh_ref, sc_ref, gt_ref, ng_ref = next(it), next(it), next(it), next(it)
    wg_ref, wu_ref, wd_ref = next(it), next(it), next(it)
    if final:
        fg_ref, yp_ref, ys_ref = next(it), next(it), next(it)
    else:
        o_ref = next(it)

    if has_mix:
        x = x + gmix_ref[...] * _dot(mix, wout_ref[...])
    h = _modulate(x, ng_ref[...], sh_ref[...], sc_ref[...]).astype(BF16)
    acc = jnp.zeros(x.shape, F32)
    for j in range(D_FF // FF_CHUNK):
        cols = slice(j * FF_CHUNK, (j + 1) * FF_CHUNK)
        a = _dot(h, wg_ref[:, cols].astype(BF16))
        u = _dot(h, wu_ref[:, cols].astype(BF16))
        acc = acc + _dot((_silu(a) * u).astype(BF16), wd_ref[cols, :].astype(BF16))
    out = x + 0.5 * gt_ref[...] * acc
    if final:
        y = _rms(out, fg_ref[...])

        @pl.when(is_ctx)
        def _():
            yp_ref[...] = y

        @pl.when(jnp.logical_not(is_ctx))
        def _():
            ys_ref[...] = y
    else:
        o_ref[...] = out


def _ffn(x, mod5, norm_g3, wg, wu, wd, l, k, *, mix=None, w_out=None, w_out_idx=None, final_g=None):
    j = 0 if k == 0 else 6
    grp = functools.partial(_group_of_block, tm=TM_FFN)
    row_spec = pl.BlockSpec((TM_FFN, D_MODEL), lambda i: (i, 0))
    split_x = isinstance(x, tuple)
    in_specs = _split_specs(D_MODEL) if split_x else [row_spec]
    args = list(x) if split_x else [x]
    if mix is not None:
        in_specs += _split_specs(D_MODEL) + [_resident((None, D_MODEL, D_MODEL), lambda i: (w_out_idx, 0, 0)),
                                             _mod_spec(l, 5, grp)]
        args += [*mix, w_out, mod5]
    in_specs += [_mod_spec(l, j, grp), _mod_spec(l, j + 1, grp), _mod_spec(l, j + 2, grp),
                 pl.BlockSpec((None, 1, D_MODEL), lambda i: (3 * l + (0 if k == 0 else 2), 0, 0)),
                 _resident((None, None, D_MODEL, D_FF), lambda i: (l, k, 0, 0)),
                 _resident((None, None, D_MODEL, D_FF), lambda i: (l, k, 0, 0)),
                 _resident((None, None, D_FF, D_MODEL), lambda i: (l, k, 0, 0))]
    args += [mod5, mod5, mod5, norm_g3, wg, wu, wd]
    if final_g is not None:
        in_specs.append(pl.BlockSpec((1, D_MODEL), lambda i: (0, 0)))
        args.append(final_g)
        out_specs = _split_specs(D_MODEL)
        out_shape = [jax.ShapeDtypeStruct((P_TOK, D_MODEL), F32), jax.ShapeDtypeStruct((S_TOK, D_MODEL), F32)]
    else:
        out_specs = row_spec
        out_shape = jax.ShapeDtypeStruct((N_TOK, D_MODEL), F32)
    return pl.pallas_call(
        functools.partial(_ffn_kernel, split_x=split_x, has_mix=mix is not None, final=final_g is not None),
        grid=(N_TOK // TM_FFN,),
        in_specs=in_specs,
        out_specs=out_specs,
        out_shape=out_shape,
        compiler_params=_cparams("arbitrary"),
        name=f"ffn_l{l}_{k}",
    )(*args)


def _rope_tables(subvectors, positions):
    t = next(iter(positions.values())).shape[0]
    cos = np.ones((t, LANES), np.float32)
    sin_m = np.zeros((t, LANES), np.float32)
    sin_p = np.zeros((t, LANES), np.float32)
    for first, n, key in subvectors:
        half = n // 2
        inv = np.float32(ROPE_BASE) ** (-np.arange(half, dtype=np.float32) * np.float32(2.0) / np.float32(n))
        ang = positions[key][:, None] * inv[None, :].astype(np.float32)
        c, s = np.cos(ang).astype(np.float32), np.sin(ang).astype(np.float32)
        cos[:, first:first + n] = np.concatenate([c, c], axis=1)
        sin_m[:, first:first + half] = -s
        sin_p[:, first + half:first + n] = s
    return jnp.asarray(cos), jnp.asarray(sin_m), jnp.asarray(sin_p)


def _tile_lanes(t, width):
    return t if width == LANES else jnp.concatenate([t] * (width // LANES), axis=1)


def _rope(y, cos, sin_m, sin_p, half):
    w = y.shape[1]
    c, sm, sp = (_tile_lanes(t, w) for t in (cos, sin_m, sin_p))
    return y * c + pltpu.roll(y, w - half, 1) * sm + pltpu.roll(y, half, 1) * sp


def _ones_lane(lane, width):
    idx = lax.broadcasted_iota(jnp.int32, (1, width), 1)
    return jnp.where((idx & (LANES - 1)) == lane, 1.0, 0.0)


def _even_proj_kernel(*refs, rope, cache):
    it = iter(refs)
    x_ref, sh_ref, sc_ref, ng_ref = next(it), next(it), next(it), next(it)
    win_ref, qn_ref, wuq_ref, kvn_ref, wk_ref, wv_ref = (next(it) for _ in range(6))
    if rope:
        tq = [next(it)[...] for _ in range(3)]
        td = [next(it)[...] for _ in range(3)]
    q_ref, k_ref, va_ref, qd_ref, kd_ref, vd_ref = (next(it) for _ in range(6))
    if cache:
        ckv_ref, krp_ref, kd32_ref, vd32_ref = (next(it) for _ in range(4))

    h = _modulate(x_ref[...], ng_ref[...], sh_ref[...], sc_ref[...]).astype(BF16)
    p = _dot(h, win_ref[...])
    o_ckv = Q_LORA
    o_krp = o_ckv + KV_LORA
    o_qd = o_krp + HEAD_W
    o_kd = o_qd + H_B * 2 * DH_B
    o_vd = o_kd + H_B * 2 * DH_B
    cqn = _rms(p[:, :Q_LORA], qn_ref[...]).astype(BF16)
    q = _dot(cqn, wuq_ref[...])
    ckvn = _rms(p[:, o_ckv:o_krp], kvn_ref[...])
    ckvb = ckvn.astype(BF16)
    krp = p[:, o_krp:o_qd]
    qd = p[:, o_qd:o_kd]
    kd = p[:, o_kd:o_vd]
    vd = p[:, o_vd:]
    if cache:
        ckv_ref[...] = ckvn
        krp_ref[...] = krp
        kd32_ref[...] = kd
        vd32_ref[...] = vd
    if rope:
        q = _rope(q, *tq, half=ROPE_A // 4)
        krp = _rope(krp, *tq, half=ROPE_A // 4)
        qd = _rope(qd, *td, half=DH_B // 4)
        kd = _rope(kd, *td, half=DH_B // 4)
    q_ref[...] = (q * (LOG2E * (NOPE_A + ROPE_A) ** -0.5)).astype(BF16)
    k_ref[...] = (_dot(ckvb, wk_ref[...]) + _tile_lanes(krp, H_A * HEAD_W)).astype(BF16)
    va = _dot(ckvb, wv_ref[...])
    if rope:
        va = va + _ones_lane(V_A, H_A * HEAD_W)
    va_ref[...] = va.astype(BF16)
    qd_ref[...] = (qd * (LOG2E * DH_B ** -0.5)).astype(BF16)
    kd_ref[...] = kd.astype(BF16)
    vd_ref[...] = vd.astype(BF16)


def _even_proj(x, mod5, norm_g3, w_in, q_norm, w_uq, kv_norm, w_k, w_v, l, *, latent, tables=None):
    i = l // 2
    n_rows = S_TOK if latent else P_TOK
    first = P_TOK // TM if latent else 0
    grp = (lambda b: 1 + b // (DEC_SEQ // TM)) if latent else (lambda b: 0)
    row_spec = lambda w: pl.BlockSpec((TM, w), lambda b: (b, 0))
    const = lambda shape: _resident((None,) + shape, lambda b: (i,) + (0,) * len(shape))
    in_specs = [pl.BlockSpec((TM, D_MODEL), lambda b: (first + b, 0)),
                _mod_spec(l, 3, grp), _mod_spec(l, 4, grp),
                pl.BlockSpec((None, 1, D_MODEL), lambda b: (3 * l + 1, 0, 0)),
                const((D_MODEL, EVEN_W)), const((1, Q_LORA)), const((Q_LORA, H_A * HEAD_W)),
                const((1, KV_LORA)), const((KV_LORA, H_A * HEAD_W)), const((KV_LORA, w_v.shape[-1]))]
    args = [x, mod5, mod5, norm_g3, w_in, q_norm, w_uq, kv_norm, w_k, w_v]
    if latent:
        in_specs += [pl.BlockSpec((TM, LANES), lambda b: (b % (DEC_SEQ // TM), 0))] * 6
        args += list(tables)
    widths = [H_A * HEAD_W, H_A * HEAD_W, w_v.shape[-1], H_B * DV_B, H_B * DV_B, H_B * DV_B]
    out_specs = [row_spec(w) for w in widths]
    out_shape = [jax.ShapeDtypeStruct((n_rows, w), BF16) for w in widths]
    if not latent:
        cache_w = [KV_LORA, HEAD_W, H_B * DV_B, H_B * DV_B]
        out_specs += [row_spec(w) for w in cache_w]
        out_shape += [jax.ShapeDtypeStruct((n_rows, w), F32) for w in cache_w]
    return pl.pallas_call(
        functools.partial(_even_proj_kernel, rope=latent, cache=not latent),
        grid=(n_rows // TM,),
        in_specs=in_specs,
        out_specs=out_specs,
        out_shape=out_shape,
        compiler_params=_cparams("parallel"),
        name=f"even_proj_l{l}_{'s' if latent else 'p'}",
    )(*args)


def _cache_kv_kernel(ckv_ref, krp_ref, wk_ref, wv_ref, k_ref, va_ref):
    c = ckv_ref[...].astype(BF16)
    k_ref[...] = (_dot(c, wk_ref[...]) + _tile_lanes(krp_ref[...], H_A * HEAD_W)).astype(BF16)
    va_ref[...] = (_dot(c, wv_ref[...]) + _ones_lane(V_A, H_A * HEAD_W)).astype(BF16)


def _cache_kv(ckv, krp, w_k, w_v, i):
    n = DEC_BATCH * PAST_LEN
    return pl.pallas_call(
        _cache_kv_kernel,
        grid=(n // PAST_LEN,),
        in_specs=[pl.BlockSpec((PAST_LEN, KV_LORA), lambda b: (b, 0)),
                  pl.BlockSpec((PAST_LEN, HEAD_W), lambda b: (b, 0)),
                  _resident((None, KV_LORA, H_A * HEAD_W), lambda b: (i, 0, 0)),
                  _resident((None, KV_LORA, H_A * HEAD_W), lambda b: (i, 0, 0))],
        out_specs=[pl.BlockSpec((PAST_LEN, H_A * HEAD_W), lambda b: (b, 0)),
                   pl.BlockSpec((PAST_LEN, H_A * HEAD_W), lambda b: (b, 0))],
        out_shape=[jax.ShapeDtypeStruct((n, H_A * HEAD_W), BF16),
                   jax.ShapeDtypeStruct((n, H_A * HEAD_W), BF16)],
        compiler_params=_cparams("parallel"),
        name=f"cache_kv_{i}",
    )(ckv, krp, w_k, w_v)


def _exp2_parts(s):
    m = functools.reduce(jnp.maximum, [jnp.max(si, axis=-1, keepdims=True) for si in s])
    return [jnp.exp2(si - m) for si in s]


def _rowsum(ps):
    return functools.reduce(jnp.add, [jnp.sum(p, axis=-1, keepdims=True) for p in ps])


def _pv(ps, vs):
    return functools.reduce(jnp.add, [_dot(p.astype(BF16), v) for p, v in zip(ps, vs)])


def _attn_kernel(*refs, lam_init, n_seg, ones_col):
    q_ref, qd_ref = refs[:2]
    segs = [refs[2 + 4 * g:6 + 4 * g] for g in range(n_seg)]
    lam_ref, sub_ref, o_ref = refs[2 + 4 * n_seg:]
    lp = lam_ref[...]
    lam = (jnp.exp(jnp.sum(lp[0:1] * lp[1:2], axis=-1, keepdims=True))
           - jnp.exp(jnp.sum(lp[2:3] * lp[3:4], axis=-1, keepdims=True)) + lam_init)
    first = lax.broadcasted_iota(jnp.int32, (1, DV_B), 1) < DH_B
    va_w = HEAD_W if ones_col else V_A

    def scores(task):
        kind, h = task
        if kind == "mla":
            cols = slice(h * HEAD_W, (h + 1) * HEAD_W)
            return ([_dot_nt(q_ref[:, cols], sg[0][:, cols]) for sg in segs],)
        cols = slice(h * DV_B, (h + 1) * DV_B)
        qh = qd_ref[:, cols].astype(F32)
        q1 = jnp.where(first, qh, 0.0).astype(BF16)
        q2 = jnp.where(first, 0.0, qh).astype(BF16)
        return tuple([_dot_nt(qx, sg[2][:, cols]) for sg in segs] for qx in (q1, q2))

    tasks = [("mla", h) for h in range(H_A)] + [("diff", h) for h in range(H_B)]
    outs = []
    ahead = scores(tasks[0])
    for t, (kind, h) in enumerate(tasks):
        cur = ahead
        if t + 1 < len(tasks):
            ahead = scores(tasks[t + 1])
        if kind == "mla":
            p = _exp2_parts(cur[0])
            acc = _pv(p, [sg[1][:, h * va_w:(h + 1) * va_w] for sg in segs])
            inv = 1.0 / (acc[:, V_A:V_A + 1] if ones_col else _rowsum(p))
            outs.append(acc[:, :V_A] * inv)
        else:
            p1, p2 = _exp2_parts(cur[0]), _exp2_parts(cur[1])
            w1, w2 = 1.0 / _rowsum(p1), lam / _rowsum(p2)
            o = _pv([a * w1 - b * w2 for a, b in zip(p1, p2)],
                    [sg[3][:, h * DV_B:(h + 1) * DV_B] for sg in segs])
            outs.append(_rms(o, sub_ref[...]) * (1.0 - lam_init))
    o_ref[...] = jnp.concatenate(outs, axis=1).astype(BF16)


def _attention(q, qd, segs, lam_p, subln, i, lam_init, *, n_batch, t_q, name):
    tq = min(TQ, t_q)
    nq = t_q // tq
    q_spec = lambda w: pl.BlockSpec((tq, w), lambda b, j: (b * nq + j, 0))
    in_specs = [q_spec(H_A * HEAD_W), q_spec(H_B * DV_B)]
    args = [q, qd]
    for seg in segs:
        t_seg = seg[0].shape[0] // n_batch
        for a in seg:
            mode = pl.Buffered(1) if nq > 1 else None
            in_specs.append(pl.BlockSpec((t_seg, a.shape[1]), lambda b, j: (b, 0), pipeline_mode=mode))
            args.append(a)
    in_specs += [pl.BlockSpec((None, 4, DH_B), lambda b, j: (i, 0, 0)),
                 pl.BlockSpec((None, 1, DV_B), lambda b, j: (i, 0, 0))]
    args += [lam_p, subln]
    ones_col = segs[0][1].shape[1] == H_A * HEAD_W
    return pl.pallas_call(
        functools.partial(_attn_kernel, lam_init=lam_init, n_seg=len(segs), ones_col=ones_col),
        grid=(n_batch, nq),
        in_specs=in_specs,
        out_specs=q_spec(D_MODEL),
        out_shape=jax.ShapeDtypeStruct((n_batch * t_q, D_MODEL), BF16),
        compiler_params=_cparams("parallel", "arbitrary"),
        name=name,
    )(*args)


def _odd_proj_kernel(*refs, rope):
    it = iter(refs)
    x_ref, sh_ref, sc_ref, ng_ref, win_ref, wgate_ref, bgate_ref = (next(it) for _ in range(7))
    if rope:
        tr = [next(it)[...] for _ in range(3)]
    o_ref = next(it)

    h = _modulate(x_ref[...], ng_ref[...], sh_ref[...], sc_ref[...]).astype(BF16)
    p = _dot(h, win_ref[...])
    o_ref[:, 0:256] = p[:, 0:256] * (DK_C ** -0.5)
    o_ref[:, 256:1536] = p[:, 256:1536]
    rq = p[:, 1536:1792]
    rk = p[:, 1792:2048] * (DK_D ** -0.5)
    if rope:
        rq = _rope(rq, *tr, half=DK_D // 2)
        rk = _rope(rk, *tr, half=DK_D // 2)
    o_ref[:, 1536:1792] = rq
    o_ref[:, 1792:2048] = rk
    o_ref[:, 2048:ODD_MAIN] = p[:, 2048:ODD_MAIN]
    logit = _dot(p[:, ODD_MAIN:].astype(BF16), wgate_ref[...]) + bgate_ref[...]
    o_ref[:, ODD_MAIN:] = (jnp.minimum(logit, 0.0) - jnp.log1p(jnp.exp(-jnp.abs(logit)))) * (1.0 / GATE_TAU)


def _odd_proj(x, mod5, norm_g3, w_in, w_gate, b_gate, l, *, latent, tables=None):
    i = l // 2
    n_rows = S_TOK if latent else P_TOK
    first = P_TOK // TM if latent else 0
    grp = (lambda b: 1 + b // (DEC_SEQ // TM)) if latent else (lambda b: 0)
    in_specs = [pl.BlockSpec((TM, D_MODEL), lambda b: (first + b, 0)),
                _mod_spec(l, 3, grp), _mod_spec(l, 4, grp),
                pl.BlockSpec((None, 1, D_MODEL), lambda b: (3 * l + 1, 0, 0)),
                _resident((None, D_MODEL, ODD_W), lambda b: (i, 0, 0)),
                _resident((None, LANES, 2 * H_C * DK_C), lambda b: (i, 0, 0)),
                _resident((None, 1, 2 * H_C * DK_C), lambda b: (i, 0, 0))]
    args = [x, mod5, mod5, norm_g3, w_in, w_gate, b_gate]
    if latent:
        in_specs += [pl.BlockSpec((TM, LANES), lambda b: (b % (DEC_SEQ // TM), 0))] * 3
        args += list(tables)
    return pl.pallas_call(
        functools.partial(_odd_proj_kernel, rope=latent),
        grid=(n_rows // TM,),
        in_specs=in_specs,
        out_specs=pl.BlockSpec((TM, ODD_OUT), lambda b: (b, 0)),
        out_shape=jax.ShapeDtypeStruct((n_rows, ODD_OUT), F32),
        compiler_params=_cparams("parallel"),
        name=f"odd_proj_l{l}_{'s' if latent else 'p'}",
    )(*args)


def _scan_weights():
    c = CHUNK
    t = np.arange(c)
    out = np.zeros((2, N_WALL * c, c), np.float32)
    for d in range(2):
        run = (t[None, :] <= t[:, None]) if d == 0 else (t[None, :] >= t[:, None])
        run = run.astype(np.float32)
        out[d, 0:c] = run
        for li, b in enumerate(LEVELS):
            base = (t // (2 * b)) * (2 * b)
            r = base + (b - 1 if d == 0 else b)
            out[d, (li + 1) * c:(li + 2) * c] = run - run[r]
        out[d, (N_WALL - 1) * c:] = 1.0 - run
    return out


C_SAME = 0
C_QM = C_SAME + len(LEVELS) + 1
C_DIST = C_QM + 3 * 2 * len(LEVELS)
C_CAUSAL = C_DIST + 1
C_CNT = C_CAUSAL + 2
N_CONST = C_CNT + 2


def _scan_consts():
    c = CHUNK
    t = np.arange(c)[:, None]
    s = np.arange(2 * c)[None, :] % c
    head0 = (np.arange(2 * c)[None, :] < c)
    out = np.zeros((N_CONST, c, 2 * c), np.float32)
    for li, b in enumerate(LEVELS):
        out[C_SAME + li] = (t // (2 * b)) == (s // (2 * b))
        later = (t % (2 * b)) >= b
        for d in range(2):
            is_q = later if d == 0 else ~later
            base = C_QM + 3 * (d * len(LEVELS) + li)
            out[base] = is_q
            out[base + 1] = ~is_q & head0
            out[base + 2] = ~is_q & ~head0
    out[C_SAME + len(LEVELS)] = (t == s)
    out[C_DIST] = np.abs(t - s)
    out[C_CAUSAL] = s <= t
    out[C_CAUSAL + 1] = s >= t
    out[C_CNT] = t + 1
    out[C_CNT + 1] = c - t
    return out


def _pair_rhs(kf, m0, m1):
    return jnp.concatenate([kf * m0, kf * m1], axis=0).astype(BF16)


def _pair_values(v):
    left = lax.broadcasted_iota(jnp.int32, (1, v.shape[1]), 1) < v.shape[1] // 2
    return jnp.concatenate([jnp.where(left, v, 0.0), jnp.where(left, 0.0, v)], axis=0).astype(BF16)


def _scan_kernel(*refs, n_chunks, n_seq, has_s0, emit_state):
    it = iter(refs)
    ins = [[next(it) for _ in range(7)] for _ in range(2)]
    wall_ref, cst_ref, bd_ref, lg_ref = next(it), next(it), next(it), next(it)
    if has_s0:
        s0g_ref, s0r_ref = next(it), next(it)
    o_refs = [[next(it), next(it)] for _ in range(2)]
    if emit_state:
        sgo_ref, sro_ref = next(it), next(it)
    sg_ref, sr_ref = next(it), next(it)
    c_id = pl.program_id(1)

    @pl.when(c_id == 0)
    def _():
        if has_s0:
            sg_ref[...] = s0g_ref[...]
            sr_ref[...] = s0r_ref[...]
        else:
            sg_ref[...] = jnp.zeros(sg_ref.shape, F32)
            sr_ref[...] = jnp.zeros(sr_ref.shape, F32)

    n_lv = len(LEVELS)
    lane = lax.broadcasted_iota(jnp.int32, (1, 2 * DK_C), 1)
    head0 = jnp.where(lane < DK_C, 1.0, 0.0)
    head1 = 1.0 - head0
    bd = bd_ref[...]
    streams = [(sq, d) for sq in range(n_seq) for d in range(2)]
    dall = {}
    for sq, d in streams:
        g = ins[d][3][sq]
        g_hi = g.astype(BF16)
        g_lo = (g - g_hi.astype(F32)).astype(BF16)
        dall[sq, d] = _dot(wall_ref[d], jnp.concatenate([g_hi, g_lo], axis=0))
    outs = {s: [] for s in streams}
    for p in range(H_C // 2):
        ln = slice(p * 2 * DK_C, (p + 1) * 2 * DK_C)
        qk = {(sq, d): (ins[d][0][sq, :, ln], ins[d][1][sq, :, ln]) for sq, d in streams}
        a = {s: _dot_nt(qk[s][0].astype(BF16), _pair_rhs(qk[s][1], head0, head1)) * cst_ref[C_SAME + n_lv]
             for s in streams}
        for li in range(n_lv):
            for s in streams:
                q, k = qk[s]
                e = jnp.exp(-jnp.abs(dall[s][(li + 1) * CHUNK:(li + 2) * CHUNK, ln]))
                base = C_QM + 3 * (s[1] * n_lv + li)
                a[s] = a[s] + _dot_nt((q * (e * cst_ref[base])).astype(BF16),
                                      _pair_rhs(k * e, cst_ref[base + 1], cst_ref[base + 2])) * cst_ref[C_SAME + li]
        for s in streams:
            sq, d = s
            q, k = qk[s]
            bc = dall[s][0:CHUNK, ln]
            tot = bc[CHUNK - 1:CHUNK] if d == 0 else bc[0:1]
            qb = (q * jnp.exp(bc)).astype(BF16)
            kb = (k * jnp.exp(dall[s][(N_WALL - 1) * CHUNK:, ln])).astype(BF16)
            v = ins[d][2][sq, :, p * 2 * DV_C:(p + 1) * 2 * DV_C]
            st = sg_ref[sq, d, p]
            outs[s].append(_dot(a[s].astype(BF16), _pair_values(v)) + _dot_nt(qb, st.astype(BF16)))
            sg_ref[sq, d, p] = st * jnp.exp(tot) + _dot_tn(v.astype(BF16), kb) * bd
    for sq, d in streams:
        o_refs[d][0][sq] = jnp.concatenate(outs[sq, d], axis=1)
    outs = {s: [] for s in streams}
    for p in range(H_D // 2):
        ln = slice(p * 2 * DK_D, (p + 1) * 2 * DK_D)
        for d in range(2):
            lg = lg_ref[d, :, ln]
            cnt = cst_ref[C_CNT + d]
            dec = jnp.exp(lg * cst_ref[C_DIST]) * cst_ref[C_CAUSAL + d]
            q_dec, k_dec, s_dec = jnp.exp(lg * cnt), jnp.exp(lg * (CHUNK - cnt)), jnp.exp(lg * CHUNK)
            for sq in range(n_seq):
                q, k = ins[d][4][sq, :, ln], ins[d][5][sq, :, ln]
                a = _dot_nt(q.astype(BF16), _pair_rhs(k, head0, head1)) * dec
                v = ins[d][6][sq, :, p * 2 * DV_D:(p + 1) * 2 * DV_D]
                st = sr_ref[sq, d, p]
                outs[sq, d].append(_dot(a.astype(BF16), _pair_values(v)) + _dot_nt((q * q_dec).astype(BF16), st.astype(BF16)))
                sr_ref[sq, d, p] = st * s_dec + _dot_tn(v.astype(BF16), (k * k_dec).astype(BF16)) * bd
    for sq, d in streams:
        o_refs[d][1][sq] = jnp.concatenate(outs[sq, d], axis=1)

    if emit_state:
        @pl.when(c_id == n_chunks - 1)
        def _():
            sgo_ref[...] = sg_ref[...]
            sro_ref[...] = sr_ref[...]


def _pair_states(s):
    n, _, h, dk, dv = s.shape
    st = jnp.swapaxes(s, -1, -2).reshape(n, 2, h // 2, 2, dv, dk)
    z = jnp.zeros_like(st[:, :, :, 0])
    top = jnp.concatenate([st[:, :, :, 0], z], axis=-1)
    bot = jnp.concatenate([z, st[:, :, :, 1]], axis=-1)
    return jnp.concatenate([top, bot], axis=-2)


def _unpair_states(sp):
    n, _, hp, dv2, dk2 = sp.shape
    dv, dk = dv2 // 2, dk2 // 2
    st = jnp.stack([sp[..., :dv, :dk], sp[..., dv:, dk:]], axis=3).reshape(n, 2, 2 * hp, dv, dk)
    return jnp.swapaxes(st, -1, -2)


def _scan(p, lg_rows, wall, cst, bd, *, n_seq, t_seq, s0=None, emit_state):
    nc = t_seq // CHUNK
    ns = min(SCAN_SEQS, n_seq)
    p3 = p.reshape(n_seq, t_seq, ODD_OUT)
    fwd = lambda blk: (lambda b, c: (b, c, blk))
    bwd = lambda blk: (lambda b, c: (b, nc - 1 - c, blk))
    in_specs, args = [], []
    for m in (fwd, bwd):
        d = 0 if m is fwd else 1
        in_specs += [pl.BlockSpec((ns, CHUNK, 256), m(0)), pl.BlockSpec((ns, CHUNK, 256), m(1)),
                     pl.BlockSpec((ns, CHUNK, 512), m(1)), pl.BlockSpec((ns, CHUNK, 256), m(ODD_MAIN // 256 + d)),
                     pl.BlockSpec((ns, CHUNK, 256), m(6)), pl.BlockSpec((ns, CHUNK, 256), m(7)),
                     pl.BlockSpec((ns, CHUNK, 512), m(4))]
        args += [p3] * 7
    in_specs += [_resident(wall.shape, lambda b, c: (0, 0, 0)), _resident(cst.shape, lambda b, c: (0, 0, 0)),
                 _resident(bd.shape, lambda b, c: (0, 0)), _resident(lg_rows.shape, lambda b, c: (0, 0, 0))]
    args += [wall, cst, bd, lg_rows]
    st_shape = (ns, 2, H_C // 2, 2 * DV_C, 2 * DK_C)
    st_spec = pl.BlockSpec(st_shape, lambda b, c: (b, 0, 0, 0, 0))
    if s0 is not None:
        in_specs += [st_spec, st_spec]
        args += list(s0)
    out_specs = [pl.BlockSpec((ns, CHUNK, 512), fwd(0)), pl.BlockSpec((ns, CHUNK, 512), fwd(0)),
                 pl.BlockSpec((ns, CHUNK, 512), bwd(0)), pl.BlockSpec((ns, CHUNK, 512), bwd(0))]
    out_shape = [jax.ShapeDtypeStruct((n_seq, t_seq, 512), F32)] * 4
    if emit_state:
        out_specs += [st_spec, st_spec]
        out_shape += [jax.ShapeDtypeStruct((n_seq,) + st_shape[1:], F32)] * 2
    res = pl.pallas_call(
        functools.partial(_scan_kernel, n_chunks=nc, n_seq=ns, has_s0=s0 is not None, emit_state=emit_state),
        grid=(n_seq // ns, nc),
        in_specs=in_specs,
        out_specs=out_specs,
        out_shape=out_shape,
        scratch_shapes=[pltpu.VMEM(st_shape, F32), pltpu.VMEM(st_shape, F32)],
        compiler_params=_cparams("parallel", "arbitrary"),
        name=f"scan_{'s' if s0 is not None else 'p'}",
    )(*args)
    return [r.reshape(n_seq * t_seq, 512) for r in res[:4]] + list(res[4:])


def _odd_post_kernel(ogf_ref, orf_ref, ogb_ref, orb_ref, gr_ref, rr_ref, gn_ref, rn_ref, o_ref):
    og = ogf_ref[...] + ogb_ref[...]
    orr = orf_ref[...] + orb_ref[...]
    gr, rr = gr_ref[...], rr_ref[...]
    outs = []
    for h in range(H_C):
        hc = slice(h * DV_C, (h + 1) * DV_C)
        outs.append(_rms(og[:, hc], gn_ref[...]) * _silu(gr[:, hc]))
    for h in range(H_D):
        hc = slice(h * DV_D, (h + 1) * DV_D)
        xh = orr[:, hc]
        mu = jnp.mean(xh, axis=-1, keepdims=True)
        xc = xh - mu
        var = jnp.mean(xc * xc, axis=-1, keepdims=True)
        outs.append(xc * lax.rsqrt(var + EPS) * rn_ref[...] * _silu(rr[:, hc]))
    o_ref[...] = jnp.concatenate(outs, axis=1).astype(BF16)


def _odd_post(scan_outs, p, gla_norm, ret_norm, i):
    n = p.shape[0]
    row = lambda blk: pl.BlockSpec((TM, 512), lambda b: (b, blk))
    return pl.pallas_call(
        _odd_post_kernel,
        grid=(n // TM,),
        in_specs=[row(0)] * 4 + [row(2), row(5),
                                 pl.BlockSpec((None, 1, DV_C), lambda b: (i, 0, 0)),
                                 pl.BlockSpec((None, 1, DV_D), lambda b: (i, 0, 0))],
        out_specs=pl.BlockSpec((TM, D_MODEL), lambda b: (b, 0)),
        out_shape=jax.ShapeDtypeStruct((n, D_MODEL), BF16),
        compiler_params=_cparams("parallel"),
        name="odd_post",
    )(*scan_outs, p, p, gla_norm, ret_norm)


def _pad_cols(w, width):
    return jnp.pad(w, [(0, 0)] * (w.ndim - 1) + [(0, width - w.shape[-1])])


def _even_weights(even_w_in, mla_w_uq, mla_w_ukv):
    cq, ckv, kr, qd, kd, vd = jnp.split(even_w_in, np.cumsum([Q_LORA, KV_LORA, ROPE_A, 512, 512]).tolist(), axis=-1)
    zeros = lambda n: jnp.zeros(even_w_in.shape[:-1] + (n,), even_w_in.dtype)
    w_in = jnp.concatenate([cq, ckv, zeros(NOPE_A), kr, zeros(HEAD_W - NOPE_A - ROPE_A), qd, kd, vd], axis=-1)
    uq = mla_w_uq.reshape(N_EVEN, Q_LORA, H_A, NOPE_A + ROPE_A)
    w_uq = _pad_cols(uq, HEAD_W).reshape(N_EVEN, Q_LORA, H_A * HEAD_W)
    ukv = mla_w_ukv.reshape(N_EVEN, KV_LORA, H_A, NOPE_A + V_A)
    w_k = _pad_cols(ukv[..., :NOPE_A], HEAD_W).reshape(N_EVEN, KV_LORA, H_A * HEAD_W)
    w_v = ukv[..., NOPE_A:].reshape(N_EVEN, KV_LORA, H_A * V_A)
    w_va = _pad_cols(ukv[..., NOPE_A:], HEAD_W).reshape(N_EVEN, KV_LORA, H_A * HEAD_W)
    return tuple(w.astype(BF16) for w in (w_in, w_uq, w_k, w_v, w_va))


def _odd_weights(odd_w_in, gla_w_gate, gla_b_gate):
    cuts = np.cumsum([256, 256, 512, 2 * GATE_RANK, 512, 256, 256, 512]).tolist()
    gq, gk, gv, glr, gr, rq, rk, rv, rr = jnp.split(odd_w_in, cuts, axis=-1)
    w_in = jnp.concatenate([gq, gk, gv, gr, rq, rk, rv, rr, _pad_cols(glr, LANES)], axis=-1)
    w_gate = jnp.zeros((N_ODD, LANES, 2 * H_C * DK_C), F32)
    for d in range(2):
        w_gate = w_gate.at[:, d * GATE_RANK:(d + 1) * GATE_RANK, d * 256:(d + 1) * 256].set(gla_w_gate[:, d])
    b_gate = gla_b_gate.reshape(N_ODD, 1, 2 * H_C * DK_C)
    return w_in.astype(BF16), w_gate.astype(BF16), b_gate


def kernel(x_prompt, x_sample, cache_mla_ckv, cache_mla_krope, cache_diff_k, cache_diff_v, state_gla, state_ret, c, c_ctx, mod_w, mod_b, norm_g, ffn_w_gate, ffn_w_up, ffn_w_down, even_w_in, mla_q_norm, mla_w_uq, mla_kv_norm, mla_w_ukv, diff_lambda, diff_subln, even_w_out, odd_w_in, gla_w_gate, gla_b_gate, gla_norm, ret_decay, ret_norm, odd_w_out, final_g):
    x = (x_prompt.reshape(P_TOK, D_MODEL), x_sample.reshape(S_TOK, D_MODEL))
    cond8 =jnp.concatenate([c_ctx[None], c, jnp.zeros((N_GROUPS - 1 - DEC_BATCH, D_MODEL), F32)], axis=0)
    mod5 = _adaln(cond8, mod_w, mod_b).reshape(DEPTH, N_MOD, N_GROUPS, 1, D_MODEL)
    norm_g3 = norm_g.reshape(DEPTH * 3, 1, D_MODEL)
    wg, wu, wd = ffn_w_gate, ffn_w_up, ffn_w_down
    e_w_in, e_w_uq, e_w_k, e_w_v, e_w_va = _even_weights(even_w_in, mla_w_uq, mla_w_ukv)
    o_w_in, o_w_gate, o_b_gate = _odd_weights(odd_w_in, gla_w_gate, gla_b_gate)
    e_w_out, o_w_out = even_w_out.astype(BF16), odd_w_out.astype(BF16)
    q_norm = mla_q_norm.reshape(N_EVEN, 1, Q_LORA)
    kv_norm = mla_kv_norm.reshape(N_EVEN, 1, KV_LORA)
    subln = diff_subln.reshape(N_EVEN, 1, DV_B)
    gla_n = gla_norm.reshape(N_ODD, 1, DV_C)
    ret_n = ret_norm.reshape(N_ODD, 1, DV_D)
    log_gamma = jnp.log1p(-jnp.exp2(-ret_decay))
    wall = jnp.asarray(np.tile(_scan_weights(), (1, 1, 2)), BF16)
    cst = jnp.asarray(_scan_consts())
    bd = jnp.asarray(np.kron(np.eye(2, dtype=np.float32), np.ones((DV_C, DK_C), np.float32)))

    tpos = np.arange(DEC_SEQ, dtype=np.float32)
    pos = {"row": np.floor(tpos / GRID_W), "col": tpos % GRID_W, "t": tpos}
    tab_q = _rope_tables([(NOPE_A, ROPE_A // 2, "row"), (NOPE_A + ROPE_A // 2, ROPE_A // 2, "col")], pos)
    tab_d = _rope_tables([(o, DH_B // 2, key) for o, key in ((0, "row"), (32, "col"), (64, "row"), (96, "col"))], pos)
    tab_r = _rope_tables([(0, DK_D, "t"), (DK_D, DK_D, "t")], pos)

    mix = None
    w_out, w_out_idx = None, None
    new_ckv, new_krope, new_dk, new_dv, new_sg, new_sr = [], [], [], [], [], []
    for l in range(DEPTH):
        i = l // 2
        x = _ffn(x, mod5, norm_g3, wg, wu, wd, l, 0)
        if l % 2 == 0:
            lam_init = 0.8 - 0.6 * math.exp(-0.3 * l)
            qp, kp, vap, qdp, kdp, vdp, ckv, krp, kd32, vd32 = _even_proj(
                x, mod5, norm_g3, e_w_in, q_norm, e_w_uq, kv_norm, e_w_k, e_w_v, l, latent=False)
            new_ckv.append(ckv.reshape(BATCH, SEQ, KV_LORA))
            new_krope.append(krp[:, NOPE_A:NOPE_A + ROPE_A].reshape(BATCH, SEQ, ROPE_A))
            new_dk.append(kd32.reshape(BATCH, SEQ, H_B, 2 * DH_B))
            new_dv.append(vd32.reshape(BATCH, SEQ, H_B, DV_B))
            mix_p = _attention(qp, qdp, [(kp, vap, kdp, vdp)],
                               diff_lambda, subln, i, lam_init, n_batch=BATCH, t_q=SEQ, name=f"attn_l{l}_p")
            qs, ks, vas, qds, kds, vds = _even_proj(
                x, mod5, norm_g3, e_w_in, q_norm, e_w_uq, kv_norm, e_w_k, e_w_va, l, latent=True,
                tables=tab_q + tab_d)
            krp_c = jnp.pad(cache_mla_krope[:, i].reshape(DEC_BATCH * PAST_LEN, ROPE_A),
                            ((0, 0), (NOPE_A, HEAD_W - NOPE_A - ROPE_A)))
            kc, vac = _cache_kv(cache_mla_ckv[:, i].reshape(DEC_BATCH * PAST_LEN, KV_LORA), krp_c, e_w_k, e_w_va, i)
            n_c = DEC_BATCH * PAST_LEN
            kdc = cache_diff_k[:, i].reshape(n_c, H_B * DV_B).astype(BF16)
            vdc = cache_diff_v[:, i].reshape(n_c, H_B * DV_B).astype(BF16)
            mix_s = _attention(qs, qds, [(kc, vac, kdc, vdc), (ks, vas, kds, vds)], diff_lambda, subln, i, lam_init,
                               n_batch=DEC_BATCH, t_q=DEC_SEQ, name=f"attn_l{l}_s")
            w_out, w_out_idx = e_w_out, i
        else:
            pp = _odd_proj(x, mod5, norm_g3, o_w_in, o_w_gate, o_b_gate, l, latent=False)
            lg_rows = jnp.repeat(log_gamma[i], DK_D, axis=-1).reshape(2, 1, H_D * DK_D)
            *outs_p, sg, sr = _scan(pp, lg_rows, wall, cst, bd, n_seq=BATCH, t_seq=SEQ, emit_state=True)
            new_sg.append(_unpair_states(sg))
            new_sr.append(_unpair_states(sr))
            mix_p = _odd_post(outs_p, pp, gla_n, ret_n, i)
            ps = _odd_proj(x, mod5, norm_g3, o_w_in, o_w_gate, o_b_gate, l, latent=True, tables=tab_r)
            s0 = (_pair_states(state_gla[:, i]), _pair_states(state_ret[:, i]))
            outs_s = _scan(ps, lg_rows, wall, cst, bd, n_seq=DEC_BATCH, t_seq=DEC_SEQ, s0=s0, emit_state=False)
            mix_s = _odd_post(outs_s, ps, gla_n, ret_n, i)
            w_out, w_out_idx = o_w_out, i
        x = _ffn(x, mod5, norm_g3, wg, wu, wd, l, 1, mix=(mix_p, mix_s), w_out=w_out, w_out_idx=w_out_idx,
                 final_g=final_g.reshape(1, D_MODEL) if l == DEPTH - 1 else None)

    y_p, y_s = x
    return (y_p.reshape(BATCH, SEQ, D_MODEL), y_s.reshape(DEC_BATCH, DEC_SEQ, D_MODEL),
            jnp.stack(new_ckv, axis=1), jnp.stack(new_krope, axis=1),
            jnp.stack(new_dk, axis=1), jnp.stack(new_dv, axis=1),
            jnp.stack(new_sg, axis=1), jnp.stack(new_sr, axis=1))
```

```python
import functools
import math

import numpy as np
import jax
import jax.numpy as jnp
from jax import lax
from jax.experimental import pallas as pl
from jax.experimental.pallas import tpu as pltpu

F32 = jnp.float32
BF16 = jnp.bfloat16

D_MODEL = 1024
BATCH = 16
SEQ = 256
DEPTH = 4
DEC_BATCH = 4
DEC_SEQ = 2048
PAST_LEN = 512
GRID_W = 64
N_EVEN = (DEPTH + 1) // 2
N_ODD = DEPTH // 2
N_MOD = 9
D_FF = 2816
H_A, NOPE_A, ROPE_A, V_A = 8, 64, 32, 64
Q_LORA, KV_LORA = 384, 256
H_B, DH_B = 4, 64
DV_B = 2 * DH_B
H_C, DK_C, DV_C = 4, 64, 128
GATE_RANK = 16
GATE_TAU = 16.0
H_D, DK_D, DV_D = 4, 64, 128
CHUNK = 64
ROPE_BASE = 10000.0
EPS = 1e-6

LANES = 128
SUBLANES = 8
VMEM_LIMIT_BYTES = 56 * 1024 * 1024

P_TOK = BATCH * SEQ
S_TOK = DEC_BATCH * DEC_SEQ
N_TOK = P_TOK + S_TOK
N_GROUPS = SUBLANES

MOD_PER_STEP = 3
TM = 512
TM_FFN = 512
FF_CHUNK = 256
TQ = 512
HEAD_W = LANES
LOG2E = math.log2(math.e)
EVEN_W = Q_LORA + KV_LORA + HEAD_W + 3 * H_B * 2 * DH_B
ODD_MAIN = 3072
ODD_W = ODD_MAIN + LANES
ODD_OUT = ODD_MAIN + 2 * H_C * DK_C
SCAN_SEQS = 4
LEVELS = (32, 16, 8, 4, 2, 1)
N_WALL = len(LEVELS) + 2


def _dot(a, b):
    return jnp.dot(a, b, preferred_element_type=F32)


def _dot_nt(a, b):
    return lax.dot_general(a, b, (((1,), (1,)), ((), ())), preferred_element_type=F32)


def _dot_tn(a, b):
    return lax.dot_general(a, b, (((0,), (0,)), ((), ())), preferred_element_type=F32)


def _silu(x):
    return x * (1.0 / (1.0 + jnp.exp(-x)))


def _rms(x, g):
    return x * lax.rsqrt(jnp.mean(x * x, axis=-1, keepdims=True) + EPS) * g


def _modulate(x, g, shift, scale):
    return _rms(x, g) * (1.0 + scale) + shift


def _group_of_block(i, tm):
    return jnp.where(i < P_TOK // tm, 0, 1 + (i - P_TOK // tm) // (DEC_SEQ // tm))


def _cparams(*sem):
    return pltpu.CompilerParams(dimension_semantics=sem, vmem_limit_bytes=VMEM_LIMIT_BYTES)


def _resident(shape, index_map):
    return pl.BlockSpec(shape, index_map, pipeline_mode=pl.Buffered(1))


def _adaln_kernel(c_ref, w_ref, b_ref, o_ref):
    s = _silu(c_ref[...]).astype(BF16)
    for k in range(MOD_PER_STEP):
        cols = slice(k * D_MODEL, (k + 1) * D_MODEL)
        o_ref[k] = _dot(s, w_ref[:, cols].astype(BF16)) + b_ref[k]


def _adaln(cond8, mod_w, mod_b):
    mod_b4 = mod_b.reshape(DEPTH, N_MOD, 1, D_MODEL)
    return pl.pallas_call(
        _adaln_kernel,
        grid=(DEPTH, N_MOD // MOD_PER_STEP),
        in_specs=[
            pl.BlockSpec((N_GROUPS, D_MODEL), lambda l, j: (0, 0)),
            pl.BlockSpec((None, D_MODEL, MOD_PER_STEP * D_MODEL), lambda l, j: (l, 0, j)),
            pl.BlockSpec((None, MOD_PER_STEP, 1, D_MODEL), lambda l, j: (l, j, 0, 0)),
        ],
        out_specs=pl.BlockSpec((None, MOD_PER_STEP, N_GROUPS, D_MODEL), lambda l, j: (l, j, 0, 0)),
        out_shape=jax.ShapeDtypeStruct((DEPTH, N_MOD, N_GROUPS, D_MODEL), F32),
        compiler_params=_cparams("arbitrary", "arbitrary"),
        name="adaln",
    )(cond8, mod_w, mod_b4)


def _mod_spec(l, j, grp):
    return pl.BlockSpec((None, None, None, 1, D_MODEL), lambda i: (l, j, grp(i), 0, 0))


def _split_specs(width):
    npb = P_TOK // TM_FFN
    return [pl.BlockSpec((TM_FFN, width), lambda i: (jnp.minimum(i, npb - 1), 0)),
            pl.BlockSpec((TM_FFN, width), lambda i: (jnp.maximum(i - npb, 0), 0))]


def _ffn_kernel(*refs, split_x, has_mix, final):
    it = iter(refs)
    is_ctx = pl.program_id(0) < P_TOK // TM_FFN
    pick = lambda p_ref, s_ref: jnp.where(is_ctx, p_ref[...], s_ref[...])
    x = pick(next(it), next(it)) if split_x else next(it)[...]
    if has_mix:
        mix = pick(next(it), next(it))
        wout_ref, gmix_ref = next(it), next(it)
    sh_ref, sc_ref, gt_ref, ng_ref = next(it), next(it), next(it), next(it)
    wg_ref, wu_ref, wd_ref = next(it), next(it), next(it)
    if final:
        fg_ref, yp_ref, ys_ref = next(it), next(it), next(it)
    else:
        o_ref = next(it)

    if has_mix:
        x = x + gmix_ref[...] * _dot(mix, wout_ref[...])
    h = _modulate(x, ng_ref[...], sh_ref[...], sc_ref[...]).astype(BF16)
    acc = jnp.zeros(x.shape, F32)
    for j in range(D_FF // FF_CHUNK):
        cols = slice(j * FF_CHUNK, (j + 1) * FF_CHUNK)
        a = _dot(h, wg_ref[:, cols].astype(BF16))
        u = _dot(h, wu_ref[:, cols].astype(BF16))
        acc = acc + _dot((_silu(a) * u).astype(BF16), wd_ref[cols, :].astype(BF16))
    out = x + 0.5 * gt_ref[...] * acc
    if final:
        y = _rms(out, fg_ref[...])

        @pl.when(is_ctx)
        def _():
            yp_ref[...] = y

        @pl.when(jnp.logical_not(is_ctx))
        def _():
            ys_ref[...] = y
    else:
        o_ref[...] = out


def _ffn(x, mod5, norm_g3, wg, wu, wd, l, k, *, mix=None, w_out=None, w_out_idx=None, final_g=None):
    j = 0 if k == 0 else 6
    grp = functools.partial(_group_of_block, tm=TM_FFN)
    row_spec = pl.BlockSpec((TM_FFN, D_MODEL), lambda i: (i, 0))
    split_x = isinstance(x, tuple)
    in_specs = _split_specs(D_MODEL) if split_x else [row_spec]
    args = list(x) if split_x else [x]
    if mix is not None:
        in_specs += _split_specs(D_MODEL) + [_resident((None, D_MODEL, D_MODEL), lambda i: (w_out_idx, 0, 0)),
                                             _mod_spec(l, 5, grp)]
        args += [*mix, w_out, mod5]
    in_specs += [_mod_spec(l, j, grp), _mod_spec(l, j + 1, grp), _mod_spec(l, j + 2, grp),
                 pl.BlockSpec((None, 1, D_MODEL), lambda i: (3 * l + (0 if k == 0 else 2), 0, 0)),
                 _resident((None, None, D_MODEL, D_FF), lambda i: (l, k, 0, 0)),
                 _resident((None, None, D_MODEL, D_FF), lambda i: (l, k, 0, 0)),
                 _resident((None, None, D_FF, D_MODEL), lambda i: (l, k, 0, 0))]
    args += [mod5, mod5, mod5, norm_g3, wg, wu, wd]
    if final_g is not None:
        in_specs.append(pl.BlockSpec((1, D_MODEL), lambda i: (0, 0)))
        args.append(final_g)
        out_specs = _split_specs(D_MODEL)
        out_shape = [jax.ShapeDtypeStruct((P_TOK, D_MODEL), F32), jax.ShapeDtypeStruct((S_TOK, D_MODEL), F32)]
    else:
        out_specs = row_spec
        out_shape = jax.ShapeDtypeStruct((N_TOK, D_MODEL), F32)
    return pl.pallas_call(
        functools.partial(_ffn_kernel, split_x=split_x, has_mix=mix is not None, final=final_g is not None),
        grid=(N_TOK // TM_FFN,),
        in_specs=in_specs,
        out_specs=out_specs,
        out_shape=out_shape,
        compiler_params=_cparams("arbitrary"),
        name=f"ffn_l{l}_{k}",
    )(*args)


def _rope_tables(subvectors, positions):
    t = next(iter(positions.values())).shape[0]
    cos = np.ones((t, LANES), np.float32)
    sin_m = np.zeros((t, LANES), np.float32)
    sin_p = np.zeros((t, LANES), np.float32)
    for first, n, key in subvectors:
        half = n // 2
        inv = np.float32(ROPE_BASE) ** (-np.arange(half, dtype=np.float32) * np.float32(2.0) / np.float32(n))
        ang = positions[key][:, None] * inv[None, :].astype(np.float32)
        c, s = np.cos(ang).astype(np.float32), np.sin(ang).astype(np.float32)
        cos[:, first:first + n] = np.concatenate([c, c], axis=1)
        sin_m[:, first:first + half] = -s
        sin_p[:, first + half:first + n] = s
    return jnp.asarray(cos), jnp.asarray(sin_m), jnp.asarray(sin_p)


def _tile_lanes(t, width):
    return t if width == LANES else jnp.concatenate([t] * (width // LANES), axis=1)


def _rope(y, cos, sin_m, sin_p, half):
    w = y.shape[1]
    c, sm, sp = (_tile_lanes(t, w) for t in (cos, sin_m, sin_p))
    return y * c + pltpu.roll(y, w - half, 1) * sm + pltpu.roll(y, half, 1) * sp


def _ones_lane(lane, width):
    idx = lax.broadcasted_iota(jnp.int32, (1, width), 1)
    return jnp.where((idx & (LANES - 1)) == lane, 1.0, 0.0)


def _even_proj_kernel(*refs, rope, cache):
    it = iter(refs)
    x_ref, sh_ref, sc_ref, ng_ref = next(it), next(it), next(it), next(it)
    win_ref, qn_ref, wuq_ref, kvn_ref, wk_ref, wv_ref = (next(it) for _ in range(6))
    if rope:
        tq = [next(it)[...] for _ in range(3)]
        td = [next(it)[...] for _ in range(3)]
    q_ref, k_ref, va_ref, qd_ref, kd_ref, vd_ref = (next(it) for _ in range(6))
    if cache:
        ckv_ref, krp_ref, kd32_ref, vd32_ref = (next(it) for _ in range(4))

    h = _modulate(x_ref[...], ng_ref[...], sh_ref[...], sc_ref[...]).astype(BF16)
    p = _dot(h, win_ref[...])
    o_ckv = Q_LORA
    o_krp = o_ckv + KV_LORA
    o_qd = o_krp + HEAD_W
    o_kd = o_qd + H_B * 2 * DH_B
    o_vd = o_kd + H_B * 2 * DH_B
    cqn = _rms(p[:, :Q_LORA], qn_ref[...]).astype(BF16)
    q = _dot(cqn, wuq_ref[...])
    ckvn = _rms(p[:, o_ckv:o_krp], kvn_ref[...])
    ckvb = ckvn.astype(BF16)
    krp = p[:, o_krp:o_qd]
    qd = p[:, o_qd:o_kd]
    kd = p[:, o_kd:o_vd]
    vd = p[:, o_vd:]
    if cache:
        ckv_ref[...] = ckvn
        krp_ref[...] = krp
        kd32_ref[...] = kd
        vd32_ref[...] = vd
    if rope:
        q = _rope(q, *tq, half=ROPE_A // 4)
        krp = _rope(krp, *tq, half=ROPE_A // 4)
        qd = _rope(qd, *td, half=DH_B // 4)
        kd = _rope(kd, *td, half=DH_B // 4)
    q_ref[...] = (q * (LOG2E * (NOPE_A + ROPE_A) ** -0.5)).astype(BF16)
    k_ref[...] = (_dot(ckvb, wk_ref[...]) + _tile_lanes(krp, H_A * HEAD_W)).astype(BF16)
    va = _dot(ckvb, wv_ref[...])
    if rope:
        va = va + _ones_lane(V_A, H_A * HEAD_W)
    va_ref[...] = va.astype(BF16)
    qd_ref[...] = (qd * (LOG2E * DH_B ** -0.5)).astype(BF16)
    kd_ref[...] = kd.astype(BF16)
    vd_ref[...] = vd.astype(BF16)


def _even_proj(x, mod5, norm_g3, w_in, q_norm, w_uq, kv_norm, w_k, w_v, l, *, latent, tables=None):
    i = l // 2
    n_rows = S_TOK if latent else P_TOK
    first = P_TOK // TM if latent else 0
    grp = (lambda b: 1 + b // (DEC_SEQ // TM)) if latent else (lambda b: 0)
    row_spec = lambda w: pl.BlockSpec((TM, w), lambda b: (b, 0))
    const = lambda shape: _resident((None,) + shape, lambda b: (i,) + (0,) * len(shape))
    in_specs = [pl.BlockSpec((TM, D_MODEL), lambda b: (first + b, 0)),
                _mod_spec(l, 3, grp), _mod_spec(l, 4, grp),
                pl.BlockSpec((None, 1, D_MODEL), lambda b: (3 * l + 1, 0, 0)),
                const((D_MODEL, EVEN_W)), const((1, Q_LORA)), const((Q_LORA, H_A * HEAD_W)),
                const((1, KV_LORA)), const((KV_LORA, H_A * HEAD_W)), const((KV_LORA, w_v.shape[-1]))]
    args = [x, mod5, mod5, norm_g3, w_in, q_norm, w_uq, kv_norm, w_k, w_v]
    if latent:
        in_specs += [pl.BlockSpec((TM, LANES), lambda b: (b % (DEC_SEQ // TM), 0))] * 6
        args += list(tables)
    widths = [H_A * HEAD_W, H_A * HEAD_W, w_v.shape[-1], H_B * DV_B, H_B * DV_B, H_B * DV_B]
    out_specs = [row_spec(w) for w in widths]
    out_shape = [jax.ShapeDtypeStruct((n_rows, w), BF16) for w in widths]
    if not latent:
        cache_w = [KV_LORA, HEAD_W, H_B * DV_B, H_B * DV_B]
        out_specs += [row_spec(w) for w in cache_w]
        out_shape += [jax.ShapeDtypeStruct((n_rows, w), F32) for w in cache_w]
    return pl.pallas_call(
        functools.partial(_even_proj_kernel, rope=latent, cache=not latent),
        grid=(n_rows // TM,),
        in_specs=in_specs,
        out_specs=out_specs,
        out_shape=out_shape,
        compiler_params=_cparams("parallel"),
        name=f"even_proj_l{l}_{'s' if latent else 'p'}",
    )(*args)


def _cache_kv_kernel(ckv_ref, krp_ref, wk_ref, wv_ref, k_ref, va_ref):
    c = ckv_ref[...].astype(BF16)
    k_ref[...] = (_dot(c, wk_ref[...]) + _tile_lanes(krp_ref[...], H_A * HEAD_W)).astype(BF16)
    va_ref[...] = (_dot(c, wv_ref[...]) + _ones_lane(V_A, H_A * HEAD_W)).astype(BF16)


def _cache_kv(ckv, krp, w_k, w_v, i):
    n = DEC_BATCH * PAST_LEN
    return pl.pallas_call(
        _cache_kv_kernel,
        grid=(n // PAST_LEN,),
        in_specs=[pl.BlockSpec((PAST_LEN, KV_LORA), lambda b: (b, 0)),
                  pl.BlockSpec((PAST_LEN, HEAD_W), lambda b: (b, 0)),
                  _resident((None, KV_LORA, H_A * HEAD_W), lambda b: (i, 0, 0)),
                  _resident((None, KV_LORA, H_A * HEAD_W), lambda b: (i, 0, 0))],
        out_specs=[pl.BlockSpec((PAST_LEN, H_A * HEAD_W), lambda b: (b, 0)),
                   pl.BlockSpec((PAST_LEN, H_A * HEAD_W), lambda b: (b, 0))],
        out_shape=[jax.ShapeDtypeStruct((n, H_A * HEAD_W), BF16),
                   jax.ShapeDtypeStruct((n, H_A * HEAD_W), BF16)],
        compiler_params=_cparams("parallel"),
        name=f"cache_kv_{i}",
    )(ckv, krp, w_k, w_v)


def _exp2_parts(s):
    m = functools.reduce(jnp.maximum, [jnp.max(si, axis=-1, keepdims=True) for si in s])
    return [jnp.exp2(si - m) for si in s]


def _rowsum(ps):
    return functools.reduce(jnp.add, [jnp.sum(p, axis=-1, keepdims=True) for p in ps])


def _pv(ps, vs):
    return functools.reduce(jnp.add, [_dot(p.astype(BF16), v) for p, v in zip(ps, vs)])


def _attn_kernel(*refs, lam_init, n_seg, ones_col):
    q_ref, qd_ref = refs[:2]
    segs = [refs[2 + 4 * g:6 + 4 * g] for g in range(n_seg)]
    lam_ref, sub_ref, o_ref = refs[2 + 4 * n_seg:]
    lp = lam_ref[...]
    lam = (jnp.exp(jnp.sum(lp[0:1] * lp[1:2], axis=-1, keepdims=True))
           - jnp.exp(jnp.sum(lp[2:3] * lp[3:4], axis=-1, keepdims=True)) + lam_init)
    first = lax.broadcasted_iota(jnp.int32, (1, DV_B), 1) < DH_B
    va_w = HEAD_W if ones_col else V_A

    def scores(task):
        kind, h = task
        if kind == "mla":
            cols = slice(h * HEAD_W, (h + 1) * HEAD_W)
            return ([_dot_nt(q_ref[:, cols], sg[0][:, cols]) for sg in segs],)
        cols = slice(h * DV_B, (h + 1) * DV_B)
        qh = qd_ref[:, cols].astype(F32)
        q1 = jnp.where(first, qh, 0.0).astype(BF16)
        q2 = jnp.where(first, 0.0, qh).astype(BF16)
        return tuple([_dot_nt(qx, sg[2][:, cols]) for sg in segs] for qx in (q1, q2))

    tasks = [("mla", h) for h in range(H_A)] + [("diff", h) for h in range(H_B)]
    outs = []
    ahead = scores(tasks[0])
    for t, (kind, h) in enumerate(tasks):
        cur = ahead
        if t + 1 < len(tasks):
            ahead = scores(tasks[t + 1])
        if kind == "mla":
            p = _exp2_parts(cur[0])
            acc = _pv(p, [sg[1][:, h * va_w:(h + 1) * va_w] for sg in segs])
            inv = 1.0 / (acc[:, V_A:V_A + 1] if ones_col else _rowsum(p))
            outs.append(acc[:, :V_A] * inv)
        else:
            p1, p2 = _exp2_parts(cur[0]), _exp2_parts(cur[1])
            w1, w2 = 1.0 / _rowsum(p1), lam / _rowsum(p2)
            o = _pv([a * w1 - b * w2 for a, b in zip(p1, p2)],
                    [sg[3][:, h * DV_B:(h + 1) * DV_B] for sg in segs])
            outs.append(_rms(o, sub_ref[...]) * (1.0 - lam_init))
    o_ref[...] = jnp.concatenate(outs, axis=1).astype(BF16)


def _attention(q, qd, segs, lam_p, subln, i, lam_init, *, n_batch, t_q, name):
    tq = min(TQ, t_q)
    nq = t_q // tq
    q_spec = lambda w: pl.BlockSpec((tq, w), lambda b, j: (b * nq + j, 0))
    in_specs = [q_spec(H_A * HEAD_W), q_spec(H_B * DV_B)]
    args = [q, qd]
    for seg in segs:
        t_seg = seg[0].shape[0] // n_batch
        for a in seg:
            mode = pl.Buffered(1) if nq > 1 else None
            in_specs.append(pl.BlockSpec((t_seg, a.shape[1]), lambda b, j: (b, 0), pipeline_mode=mode))
            args.append(a)
    in_specs += [pl.BlockSpec((None, 4, DH_B), lambda b, j: (i, 0, 0)),
                 pl.BlockSpec((None, 1, DV_B), lambda b, j: (i, 0, 0))]
    args += [lam_p, subln]
    ones_col = segs[0][1].shape[1] == H_A * HEAD_W
    return pl.pallas_call(
        functools.partial(_attn_kernel, lam_init=lam_init, n_seg=len(segs), ones_col=ones_col),
        grid=(n_batch, nq),
        in_specs=in_specs,
        out_specs=q_spec(D_MODEL),
        out_shape=jax.ShapeDtypeStruct((n_batch * t_q, D_MODEL), BF16),
        compiler_params=_cparams("parallel", "arbitrary"),
        name=name,
    )(*args)


def _odd_proj_kernel(*refs, rope):
    it = iter(refs)
    x_ref, sh_ref, sc_ref, ng_ref, win_ref, wgate_ref, bgate_ref = (next(it) for _ in range(7))
    if rope:
        tr = [next(it)[...] for _ in range(3)]
    o_ref = next(it)

    h = _modulate(x_ref[...], ng_ref[...], sh_ref[...], sc_ref[...]).astype(BF16)
    p = _dot(h, win_ref[...])
    o_ref[:, 0:256] = p[:, 0:256] * (DK_C ** -0.5)
    o_ref[:, 256:1536] = p[:, 256:1536]
    rq = p[:, 1536:1792]
    rk = p[:, 1792:2048] * (DK_D ** -0.5)
    if rope:
        rq = _rope(rq, *tr, half=DK_D // 2)
        rk = _rope(rk, *tr, half=DK_D // 2)
    o_ref[:, 1536:1792] = rq
    o_ref[:, 1792:2048] = rk
    o_ref[:, 2048:ODD_MAIN] = p[:, 2048:ODD_MAIN]
    logit = _dot(p[:, ODD_MAIN:].astype(BF16), wgate_ref[...]) + bgate_ref[...]
    o_ref[:, ODD_MAIN:] = (jnp.minimum(logit, 0.0) - jnp.log1p(jnp.exp(-jnp.abs(logit)))) * (1.0 / GATE_TAU)


def _odd_proj(x, mod5, norm_g3, w_in, w_gate, b_gate, l, *, latent, tables=None):
    i = l // 2
    n_rows = S_TOK if latent else P_TOK
    first = P_TOK // TM if latent else 0
    grp = (lambda b: 1 + b // (DEC_SEQ // TM)) if latent else (lambda b: 0)
    in_specs = [pl.BlockSpec((TM, D_MODEL), lambda b: (first + b, 0)),
                _mod_spec(l, 3, grp), _mod_spec(l, 4, grp),
                pl.BlockSpec((None, 1, D_MODEL), lambda b: (3 * l + 1, 0, 0)),
                _resident((None, D_MODEL, ODD_W), lambda b: (i, 0, 0)),
                _resident((None, LANES, 2 * H_C * DK_C), lambda b: (i, 0, 0)),
                _resident((None, 1, 2 * H_C * DK_C), lambda b: (i, 0, 0))]
    args = [x, mod5, mod5, norm_g3, w_in, w_gate, b_gate]
    if latent:
        in_specs += [pl.BlockSpec((TM, LANES), lambda b: (b % (DEC_SEQ // TM), 0))] * 3
        args += list(tables)
    return pl.pallas_call(
        functools.partial(_odd_proj_kernel, rope=latent),
        grid=(n_rows // TM,),
        in_specs=in_specs,
        out_specs=pl.BlockSpec((TM, ODD_OUT), lambda b: (b, 0)),
        out_shape=jax.ShapeDtypeStruct((n_rows, ODD_OUT), F32),
        compiler_params=_cparams("parallel"),
        name=f"odd_proj_l{l}_{'s' if latent else 'p'}",
    )(*args)


def _scan_weights():
    c = CHUNK
    t = np.arange(c)
    out = np.zeros((2, N_WALL * c, c), np.float32)
    for d in range(2):
        run = (t[None, :] <= t[:, None]) if d == 0 else (t[None, :] >= t[:, None])
        run = run.astype(np.float32)
        out[d, 0:c] = run
        for li, b in enumerate(LEVELS):
            base = (t // (2 * b)) * (2 * b)
            r = base + (b - 1 if d == 0 else b)
            out[d, (li + 1) * c:(li + 2) * c] = run - run[r]
        out[d, (N_WALL - 1) * c:] = 1.0 - run
    return out


C_SAME = 0
C_QM = C_SAME + len(LEVELS) + 1
C_DIST = C_QM + 3 * 2 * len(LEVELS)
C_CAUSAL = C_DIST + 1
C_CNT = C_CAUSAL + 2
N_CONST = C_CNT + 2


def _scan_consts():
    c = CHUNK
    t = np.arange(c)[:, None]
    s = np.arange(2 * c)[None, :] % c
    head0 = (np.arange(2 * c)[None, :] < c)
    out = np.zeros((N_CONST, c, 2 * c), np.float32)
    for li, b in enumerate(LEVELS):
        out[C_SAME + li] = (t // (2 * b)) == (s // (2 * b))
        later = (t % (2 * b)) >= b
        for d in range(2):
            is_q = later if d == 0 else ~later
            base = C_QM + 3 * (d * len(LEVELS) + li)
            out[base] = is_q
            out[base + 1] = ~is_q & head0
            out[base + 2] = ~is_q & ~head0
    out[C_SAME + len(LEVELS)] = (t == s)
    out[C_DIST] = np.abs(t - s)
    out[C_CAUSAL] = s <= t
    out[C_CAUSAL + 1] = s >= t
    out[C_CNT] = t + 1
    out[C_CNT + 1] = c - t
    return out


def _pair_rhs(kf, m0, m1):
    return jnp.concatenate([kf * m0, kf * m1], axis=0).astype(BF16)


def _pair_values(v):
    left = lax.broadcasted_iota(jnp.int32, (1, v.shape[1]), 1) < v.shape[1] // 2
    return jnp.concatenate([jnp.where(left, v, 0.0), jnp.where(left, 0.0, v)], axis=0).astype(BF16)


def _scan_kernel(*refs, n_chunks, n_seq, has_s0, emit_state):
    it = iter(refs)
    ins = [[next(it) for _ in range(7)] for _ in range(2)]
    wall_ref, cst_ref, bd_ref, lg_ref = next(it), next(it), next(it), next(it)
    if has_s0:
        s0g_ref, s0r_ref = next(it), next(it)
    o_refs = [[next(it), next(it)] for _ in range(2)]
    if emit_state:
        sgo_ref, sro_ref = next(it), next(it)
    sg_ref, sr_ref = next(it), next(it)
    c_id = pl.program_id(1)

    @pl.when(c_id == 0)
    def _():
        if has_s0:
            sg_ref[...] = s0g_ref[...]
            sr_ref[...] = s0r_ref[...]
        else:
            sg_ref[...] = jnp.zeros(sg_ref.shape, F32)
            sr_ref[...] = jnp.zeros(sr_ref.shape, F32)

    n_lv = len(LEVELS)
    lane = lax.broadcasted_iota(jnp.int32, (1, 2 * DK_C), 1)
    head0 = jnp.where(lane < DK_C, 1.0, 0.0)
    head1 = 1.0 - head0
    bd = bd_ref[...]
    streams = [(sq, d) for sq in range(n_seq) for d in range(2)]
    dall = {}
    for sq, d in streams:
        g = ins[d][3][sq]
        g_hi = g.astype(BF16)
        g_lo = (g - g_hi.astype(F32)).astype(BF16)
        dall[sq, d] = _dot(wall_ref[d], jnp.concatenate([g_hi, g_lo], axis=0))
    outs = {s: [] for s in streams}
    for p in range(H_C // 2):
        ln = slice(p * 2 * DK_C, (p + 1) * 2 * DK_C)
        qk = {(sq, d): (ins[d][0][sq, :, ln], ins[d][1][sq, :, ln]) for sq, d in streams}
        a = {s: _dot_nt(qk[s][0].astype(BF16), _pair_rhs(qk[s][1], head0, head1)) * cst_ref[C_SAME + n_lv]
             for s in streams}
        for li in range(n_lv):
            for s in streams:
                q, k = qk[s]
                e = jnp.exp(-jnp.abs(dall[s][(li + 1) * CHUNK:(li + 2) * CHUNK, ln]))
                base = C_QM + 3 * (s[1] * n_lv + li)
                a[s] = a[s] + _dot_nt((q * (e * cst_ref[base])).astype(BF16),
                                      _pair_rhs(k * e, cst_ref[base + 1], cst_ref[base + 2])) * cst_ref[C_SAME + li]
        for s in streams:
            sq, d = s
            q, k = qk[s]
            bc = dall[s][0:CHUNK, ln]
            tot = bc[CHUNK - 1:CHUNK] if d == 0 else bc[0:1]
            qb = (q * jnp.exp(bc)).astype(BF16)
            kb = (k * jnp.exp(dall[s][(N_WALL - 1) * CHUNK:, ln])).astype(BF16)
            v = ins[d][2][sq, :, p * 2 * DV_C:(p + 1) * 2 * DV_C]
            st = sg_ref[sq, d, p]
            outs[s].append(_dot(a[s].astype(BF16), _pair_values(v)) + _dot_nt(qb, st.astype(BF16)))
            sg_ref[sq, d, p] = st * jnp.exp(tot) + _dot_tn(v.astype(BF16), kb) * bd
    for sq, d in streams:
        o_refs[d][0][sq] = jnp.concatenate(outs[sq, d], axis=1)
    outs = {s: [] for s in streams}
    for p in range(H_D // 2):
        ln = slice(p * 2 * DK_D, (p + 1) * 2 * DK_D)
        for d in range(2):
            lg = lg_ref[d, :, ln]
            cnt = cst_ref[C_CNT + d]
            dec = jnp.exp(lg * cst_ref[C_DIST]) * cst_ref[C_CAUSAL + d]
            q_dec, k_dec, s_dec = jnp.exp(lg * cnt), jnp.exp(lg * (CHUNK - cnt)), jnp.exp(lg * CHUNK)
            for sq in range(n_seq):
                q, k = ins[d][4][sq, :, ln], ins[d][5][sq, :, ln]
                a = _dot_nt(q.astype(BF16), _pair_rhs(k, head0, head1)) * dec
                v = ins[d][6][sq, :, p * 2 * DV_D:(p + 1) * 2 * DV_D]
                st = sr_ref[sq, d, p]
                outs[sq, d].append(_dot(a.astype(BF16), _pair_values(v)) + _dot_nt((q * q_dec).astype(BF16), st.astype(BF16)))
                sr_ref[sq, d, p] = st * s_dec + _dot_tn(v.astype(BF16), (k * k_dec).astype(BF16)) * bd
    for sq, d in streams:
        o_refs[d][1][sq] = jnp.concatenate(outs[sq, d], axis=1)

    if emit_state:
        @pl.when(c_id == n_chunks - 1)
        def _():
            sgo_ref[...] = sg_ref[...]
            sro_ref[...] = sr_ref[...]


def _pair_states(s):
    n, _, h, dk, dv = s.shape
    st = jnp.swapaxes(s, -1, -2).reshape(n, 2, h // 2, 2, dv, dk)
    z = jnp.zeros_like(st[:, :, :, 0])
    top = jnp.concatenate([st[:, :, :, 0], z], axis=-1)
    bot = jnp.concatenate([z, st[:, :, :, 1]], axis=-1)
    return jnp.concatenate([top, bot], axis=-2)


def _unpair_states(sp):
    n, _, hp, dv2, dk2 = sp.shape
    dv, dk = dv2 // 2, dk2 // 2
    st = jnp.stack([sp[..., :dv, :dk], sp[..., dv:, dk:]], axis=3).reshape(n, 2, 2 * hp, dv, dk)
    return jnp.swapaxes(st, -1, -2)


def _scan(p, lg_rows, wall, cst, bd, *, n_seq, t_seq, s0=None, emit_state):
    nc = t_seq // CHUNK
    ns = min(SCAN_SEQS, n_seq)
    p3 = p.reshape(n_seq, t_seq, ODD_OUT)
    fwd = lambda blk: (lambda b, c: (b, c, blk))
    bwd = lambda blk: (lambda b, c: (b, nc - 1 - c, blk))
    in_specs, args = [], []
    for m in (fwd, bwd):
        d = 0 if m is fwd else 1
        in_specs += [pl.BlockSpec((ns, CHUNK, 256), m(0)), pl.BlockSpec((ns, CHUNK, 256), m(1)),
                     pl.BlockSpec((ns, CHUNK, 512), m(1)), pl.BlockSpec((ns, CHUNK, 256), m(ODD_MAIN // 256 + d)),
                     pl.BlockSpec((ns, CHUNK, 256), m(6)), pl.BlockSpec((ns, CHUNK, 256), m(7)),
                     pl.BlockSpec((ns, CHUNK, 512), m(4))]
        args += [p3] * 7
    in_specs += [_resident(wall.shape, lambda b, c: (0, 0, 0)), _resident(cst.shape, lambda b, c: (0, 0, 0)),
                 _resident(bd.shape, lambda b, c: (0, 0)), _resident(lg_rows.shape, lambda b, c: (0, 0, 0))]
    args += [wall, cst, bd, lg_rows]
    st_shape = (ns, 2, H_C // 2, 2 * DV_C, 2 * DK_C)
    st_spec = pl.BlockSpec(st_shape, lambda b, c: (b, 0, 0, 0, 0))
    if s0 is not None:
        in_specs += [st_spec, st_spec]
        args += list(s0)
    out_specs = [pl.BlockSpec((ns, CHUNK, 512), fwd(0)), pl.BlockSpec((ns, CHUNK, 512), fwd(0)),
                 pl.BlockSpec((ns, CHUNK, 512), bwd(0)), pl.BlockSpec((ns, CHUNK, 512), bwd(0))]
    out_shape = [jax.ShapeDtypeStruct((n_seq, t_seq, 512), F32)] * 4
    if emit_state:
        out_specs += [st_spec, st_spec]
        out_shape += [jax.ShapeDtypeStruct((n_seq,) + st_shape[1:], F32)] * 2
    res = pl.pallas_call(
        functools.partial(_scan_kernel, n_chunks=nc, n_seq=ns, has_s0=s0 is not None, emit_state=emit_state),
        grid=(n_seq // ns, nc),
        in_specs=in_specs,
        out_specs=out_specs,
        out_shape=out_shape,
        scratch_shapes=[pltpu.VMEM(st_shape, F32), pltpu.VMEM(st_shape, F32)],
        compiler_params=_cparams("parallel", "arbitrary"),
        name=f"scan_{'s' if s0 is not None else 'p'}",
    )(*args)
    return [r.reshape(n_seq * t_seq, 512) for r in res[:4]] + list(res[4:])


def _odd_post_kernel(ogf_ref, orf_ref, ogb_ref, orb_ref, gr_ref, rr_ref, gn_ref, rn_ref, o_ref):
    og = ogf_ref[...] + ogb_ref[...]
    orr = orf_ref[...] + orb_ref[...]
    gr, rr = gr_ref[...], rr_ref[...]
    outs = []
    for h in range(H_C):
        hc = slice(h * DV_C, (h + 1) * DV_C)
        outs.append(_rms(og[:, hc], gn_ref[...]) * _silu(gr[:, hc]))
    for h in range(H_D):
        hc = slice(h * DV_D, (h + 1) * DV_D)
        xh = orr[:, hc]
        mu = jnp.mean(xh, axis=-1, keepdims=True)
        xc = xh - mu
        var = jnp.mean(xc * xc, axis=-1, keepdims=True)
        outs.append(xc * lax.rsqrt(var + EPS) * rn_ref[...] * _silu(rr[:, hc]))
    o_ref[...] = jnp.concatenate(outs, axis=1).astype(BF16)


def _odd_post(scan_outs, p, gla_norm, ret_norm, i):
    n = p.shape[0]
    row = lambda blk: pl.BlockSpec((TM, 512), lambda b: (b, blk))
    return pl.pallas_call(
        _odd_post_kernel,
        grid=(n // TM,),
        in_specs=[row(0)] * 4 + [row(2), row(5),
                                 pl.BlockSpec((None, 1, DV_C), lambda b: (i, 0, 0)),
                                 pl.BlockSpec((None, 1, DV_D), lambda b: (i, 0, 0))],
        out_specs=pl.BlockSpec((TM, D_MODEL), lambda b: (b, 0)),
        out_shape=jax.ShapeDtypeStruct((n, D_MODEL), BF16),
        compiler_params=_cparams("parallel"),
        name="odd_post",
    )(*scan_outs, p, p, gla_norm, ret_norm)


def _pad_cols(w, width):
    return jnp.pad(w, [(0, 0)] * (w.ndim - 1) + [(0, width - w.shape[-1])])


def _even_weights(even_w_in, mla_w_uq, mla_w_ukv):
    cq, ckv, kr, qd, kd, vd = jnp.split(even_w_in, np.cumsum([Q_LORA, KV_LORA, ROPE_A, 512, 512]).tolist(), axis=-1)
    zeros = lambda n: jnp.zeros(even_w_in.shape[:-1] + (n,), even_w_in.dtype)
    w_in = jnp.concatenate([cq, ckv, zeros(NOPE_A), kr, zeros(HEAD_W - NOPE_A - ROPE_A), qd, kd, vd], axis=-1)
    uq = mla_w_uq.reshape(N_EVEN, Q_LORA, H_A, NOPE_A + ROPE_A)
    w_uq = _pad_cols(uq, HEAD_W).reshape(N_EVEN, Q_LORA, H_A * HEAD_W)
    ukv = mla_w_ukv.reshape(N_EVEN, KV_LORA, H_A, NOPE_A + V_A)
    w_k = _pad_cols(ukv[..., :NOPE_A], HEAD_W).reshape(N_EVEN, KV_LORA, H_A * HEAD_W)
    w_v = ukv[..., NOPE_A:].reshape(N_EVEN, KV_LORA, H_A * V_A)
    w_va = _pad_cols(ukv[..., NOPE_A:], HEAD_W).reshape(N_EVEN, KV_LORA, H_A * HEAD_W)
    return tuple(w.astype(BF16) for w in (w_in, w_uq, w_k, w_v, w_va))


def _odd_weights(odd_w_in, gla_w_gate, gla_b_gate):
    cuts = np.cumsum([256, 256, 512, 2 * GATE_RANK, 512, 256, 256, 512]).tolist()
    gq, gk, gv, glr, gr, rq, rk, rv, rr = jnp.split(odd_w_in, cuts, axis=-1)
    w_in = jnp.concatenate([gq, gk, gv, gr, rq, rk, rv, rr, _pad_cols(glr, LANES)], axis=-1)
    w_gate = jnp.zeros((N_ODD, LANES, 2 * H_C * DK_C), F32)
    for d in range(2):
        w_gate = w_gate.at[:, d * GATE_RANK:(d + 1) * GATE_RANK, d * 256:(d + 1) * 256].set(gla_w_gate[:, d])
    b_gate = gla_b_gate.reshape(N_ODD, 1, 2 * H_C * DK_C)
    return w_in.astype(BF16), w_gate.astype(BF16), b_gate


def kernel(x_prompt, x_sample, cache_mla_ckv, cache_mla_krope, cache_diff_k, cache_diff_v, state_gla, state_ret, c, c_ctx, mod_w, mod_b, norm_g, ffn_w_gate, ffn_w_up, ffn_w_down, even_w_in, mla_q_norm, mla_w_uq, mla_kv_norm, mla_w_ukv, diff_lambda, diff_subln, even_w_out, odd_w_in, gla_w_gate, gla_b_gate, gla_norm, ret_decay, ret_norm, odd_w_out, final_g):
    x = (x_prompt.reshape(P_TOK, D_MODEL), x_sample.reshape(S_TOK, D_MODEL))
    cond8 =jnp.concatenate([c_ctx[None], c, jnp.zeros((N_GROUPS - 1 - DEC_BATCH, D_MODEL), F32)], axis=0)
    mod5 = _adaln(cond8, mod_w, mod_b).reshape(DEPTH, N_MOD, N_GROUPS, 1, D_MODEL)
    norm_g3 = norm_g.reshape(DEPTH * 3, 1, D_MODEL)
    wg, wu, wd = ffn_w_gate, ffn_w_up, ffn_w_down
    e_w_in, e_w_uq, e_w_k, e_w_v, e_w_va = _even_weights(even_w_in, mla_w_uq, mla_w_ukv)
    o_w_in, o_w_gate, o_b_gate = _odd_weights(odd_w_in, gla_w_gate, gla_b_gate)
    e_w_out, o_w_out = even_w_out.astype(BF16), odd_w_out.astype(BF16)
    q_norm = mla_q_norm.reshape(N_EVEN, 1, Q_LORA)
    kv_norm = mla_kv_norm.reshape(N_EVEN, 1, KV_LORA)
    subln = diff_subln.reshape(N_EVEN, 1, DV_B)
    gla_n = gla_norm.reshape(N_ODD, 1, DV_C)
    ret_n = ret_norm.reshape(N_ODD, 1, DV_D)
    log_gamma = jnp.log1p(-jnp.exp2(-ret_decay))
    wall = jnp.asarray(np.tile(_scan_weights(), (1, 1, 2)), BF16)
    cst = jnp.asarray(_scan_consts())
    bd = jnp.asarray(np.kron(np.eye(2, dtype=np.float32), np.ones((DV_C, DK_C), np.float32)))

    tpos = np.arange(DEC_SEQ, dtype=np.float32)
    pos = {"row": np.floor(tpos / GRID_W), "col": tpos % GRID_W, "t": tpos}
    tab_q = _rope_tables([(NOPE_A, ROPE_A // 2, "row"), (NOPE_A + ROPE_A // 2, ROPE_A // 2, "col")], pos)
    tab_d = _rope_tables([(o, DH_B // 2, key) for o, key in ((0, "row"), (32, "col"), (64, "row"), (96, "col"))], pos)
    tab_r = _rope_tables([(0, DK_D, "t"), (DK_D, DK_D, "t")], pos)

    new_ckv, new_krope, new_dk, new_dv, new_sg, new_sr = [], [], [], [], [], []
    for l in range(DEPTH):
        i = l // 2
        x = _ffn(x, mod5, norm_g3, wg, wu, wd, l, 0)
        if l % 2 == 0:
            lam_init = 0.8 - 0.6 * math.exp(-0.3 * l)
            qp, kp, vap, qdp, kdp, vdp, ckv, krp, kd32, vd32 = _even_proj(
                x, mod5, norm_g3, e_w_in, q_norm, e_w_uq, kv_norm, e_w_k, e_w_v, l, latent=False)
            new_ckv.append(ckv.reshape(BATCH, SEQ, KV_LORA))
            new_krope.append(krp[:, NOPE_A:NOPE_A + ROPE_A].reshape(BATCH, SEQ, ROPE_A))
            new_dk.append(kd32.reshape(BATCH, SEQ, H_B, 2 * DH_B))
            new_dv.append(vd32.reshape(BATCH, SEQ, H_B, DV_B))
            mix_p = _attention(qp, qdp, [(kp, vap, kdp, vdp)],
                               diff_lambda, subln, i, lam_init, n_batch=BATCH, t_q=SEQ, name=f"attn_l{l}_p")
            qs, ks, vas, qds, kds, vds = _even_proj(
                x, mod5, norm_g3, e_w_in, q_norm, e_w_uq, kv_norm, e_w_k, e_w_va, l, latent=True,
                tables=tab_q + tab_d)
            krp_c = jnp.pad(cache_mla_krope[:, i].reshape(DEC_BATCH * PAST_LEN, ROPE_A),
                            ((0, 0), (NOPE_A, HEAD_W - NOPE_A - ROPE_A)))
            kc, vac = _cache_kv(cache_mla_ckv[:, i].reshape(DEC_BATCH * PAST_LEN, KV_LORA), krp_c, e_w_k, e_w_va, i)
            n_c = DEC_BATCH * PAST_LEN
            kdc = cache_diff_k[:, i].reshape(n_c, H_B * DV_B).astype(BF16)
            vdc = cache_diff_v[:, i].reshape(n_c, H_B * DV_B).astype(BF16)
            mix_s = _attention(qs, qds, [(kc, vac, kdc, vdc), (ks, vas, kds, vds)], diff_lambda, subln, i, lam_init,
                               n_batch=DEC_BATCH, t_q=DEC_SEQ, name=f"attn_l{l}_s")
            w_out, w_out_idx = e_w_out, i
        else:
            pp = _odd_proj(x, mod5, norm_g3, o_w_in, o_w_gate, o_b_gate, l, latent=False)
            lg_rows = jnp.repeat(log_gamma[i], DK_D, axis=-1).reshape(2, 1, H_D * DK_D)
            *outs_p, sg, sr = _scan(pp, lg_rows, wall, cst, bd, n_seq=BATCH, t_seq=SEQ, emit_state=True)
            new_sg.append(_unpair_states(sg))
            new_sr.append(_unpair_states(sr))
            mix_p = _odd_post(outs_p, pp, gla_n, ret_n, i)
            ps = _odd_proj(x, mod5, norm_g3, o_w_in, o_w_gate, o_b_gate, l, latent=True, tables=tab_r)
            s0 = (_pair_states(state_gla[:, i]), _pair_states(state_ret[:, i]))
            outs_s = _scan(ps, lg_rows, wall, cst, bd, n_seq=DEC_BATCH, t_seq=DEC_SEQ, s0=s0, emit_state=False)
            mix_s = _odd_post(outs_s, ps, gla_n, ret_n, i)
            w_out, w_out_idx = o_w_out, i
        x = _ffn(x, mod5, norm_g3, wg, wu, wd, l, 1, mix=(mix_p, mix_s), w_out=w_out, w_out_idx=w_out_idx,
                 final_g=final_g.reshape(1, D_MODEL) if l == DEPTH - 1 else None)

    y_p, y_s = x
    return (y_p.reshape(BATCH, SEQ, D_MODEL), y_s.reshape(DEC_BATCH, DEC_SEQ, D_MODEL),
            jnp.stack(new_ckv, axis=1), jnp.stack(new_krope, axis=1),
            jnp.stack(new_dk, axis=1), jnp.stack(new_dv, axis=1),
            jnp.stack(new_sg, axis=1), jnp.stack(new_sr, axis=1))
```
